```python
import jax, jax.numpy as jnp
from jax import lax
import numpy as np

D_MODEL = 2048
BATCH = 4
SEQ = 4096
DEPTH = 1

EPS = 1e-6
N_MOD = 6
CONV_CHANNELS = D_MODEL // 2
CONV_TAPS = 31
N_HEADS = 8
N_KV_HEADS = 2
HEAD_DIM = 128
GROUP = N_HEADS // N_KV_HEADS
ATTN_WIDTH = N_HEADS * HEAD_DIM
ROPE_THETA = 10000.0
Q_BLOCK = 128
IDX_HEADS = 16
IDX_DIM = 64
IDX_TOPK_MAX = 256
N_BRANCHES = 2
N_EXPERTS = 32
TOP_K = 4
D_EXPERT = D_MODEL
SWIGLU_ALPHA = 1.702
SWIGLU_LIMIT = 7.0
MOE_BLOCK = 128
COLS = (2 * CONV_CHANNELS,
        ATTN_WIDTH,
        N_KV_HEADS * HEAD_DIM,
        N_KV_HEADS * HEAD_DIM,
        IDX_HEADS * IDX_DIM,
        IDX_DIM,
        IDX_HEADS,
        N_BRANCHES * D_MODEL)
D_IN = sum(COLS)

kernel_name = "hybrid_conv_dsa_moe_block"


def rms_norm(x, g):
    xf = x.astype(jnp.float32)
    y = xf * lax.rsqrt(jnp.mean(xf * xf, axis=-1, keepdims=True) + EPS)
    return (y * g.astype(jnp.float32)).astype(x.dtype)


def layer_norm(x, g, b):
    xf = x.astype(jnp.float32)
    mu = jnp.mean(xf, axis=-1, keepdims=True)
    xc = xf - mu
    var = jnp.mean(xc * xc, axis=-1, keepdims=True)
    return (xc * lax.rsqrt(var + EPS) * g.astype(jnp.float32) + b.astype(jnp.float32)).astype(x.dtype)


def rope(x, pos):
    d = x.shape[-1]
    inv = ROPE_THETA ** (-jnp.arange(0, d, 2, dtype=jnp.float32) / d)
    ang = pos.astype(jnp.float32)[:, None] * inv[None, :]
    cos = jnp.cos(ang)[None, :, None, :]
    sin = jnp.sin(ang)[None, :, None, :]
    xf = x.astype(jnp.float32)
    x1, x2 = xf[..., : d // 2], xf[..., d // 2:]
    return jnp.concatenate([x1 * cos - x2 * sin, x2 * cos + x1 * sin], axis=-1).astype(x.dtype)


def split_cols(t):
    out, start = [], 0
    for w in COLS:
        out.append(t[..., start:start + w])
        start += w
    return out


def conv_branch(a, conv_w, conv_b, ln_g, ln_b, w_proj):
    u, gate = a[..., :CONV_CHANNELS], a[..., CONV_CHANNELS:]
    u = u * jax.nn.sigmoid(gate)
    u = jnp.pad(u, ((0, 0), (CONV_TAPS - 1, 0), (0, 0)))
    u = lax.conv_general_dilated(u, conv_w[:, None, :], window_strides=(1,), padding='VALID',
                                 dimension_numbers=('NWC', 'WIO', 'NWC'),
                                 feature_group_count=CONV_CHANNELS) + conv_b
    u = jax.nn.silu(layer_norm(u, ln_g, ln_b))
    return u @ w_proj


def sparse_attention(q, k, v, iq, ik, iw):
    B, S = q.shape[0], q.shape[1]
    n_sel = min(IDX_TOPK_MAX, S // 4)
    nb = S // Q_BLOCK
    key_pos = jnp.arange(S)
    ik32 = ik.astype(jnp.float32)
    gather = jax.vmap(lambda t, i: t[i])

    def blocks(t):
        return jnp.moveaxis(t.reshape(B, nb, Q_BLOCK, *t.shape[2:]), 1, 0)

    def one_block(args):
        qb, iqb, iwb, start = args
        q_pos = start + jnp.arange(Q_BLOCK)
        causal = key_pos[None, :] <= q_pos[:, None]
        s = jnp.einsum('bqhd,bsd->bqhs', iqb.astype(jnp.float32), ik32) * (IDX_DIM ** -0.5)
        score = jnp.einsum('bqhs,bqh->bqs', jax.nn.relu(s), iwb.astype(jnp.float32))
        score = jnp.where(causal[None], score, -jnp.inf)
        _, sel = lax.top_k(score, n_sel)
        valid = sel <= q_pos[None, :, None]
        ks = gather(k, sel)
        vs = gather(v, sel)
        qg = qb.reshape(B, Q_BLOCK, N_KV_HEADS, GROUP, HEAD_DIM)
        logits = jnp.einsum('bqkgd,bqskd->bqkgs', qg, ks,
                            preferred_element_type=jnp.float32) * (HEAD_DIM ** -0.5)
        logits = jnp.where(valid[:, :, None, None, :], logits, -jnp.inf)
        p = jax.nn.softmax(logits, axis=-1)
        o = jnp.einsum('bqkgs,bqskd->bqkgd', p.astype(vs.dtype), vs)
        return o.reshape(B, Q_BLOCK, ATTN_WIDTH)

    starts = jnp.arange(nb) * Q_BLOCK
    out = lax.map(one_block, (blocks(q), blocks(iq), blocks(iw), starts))
    return jnp.moveaxis(out, 0, 1).reshape(B, S, ATTN_WIDTH)


def moe(h, w_router, b_router, w_gate, b_gate, w_up, b_up, w_down, b_down):
    N, D = h.shape
    logits = jnp.dot(h, w_router, preferred_element_type=jnp.float32) + b_router.astype(jnp.float32)
    top_val, top_idx = lax.top_k(logits, TOP_K)
    probs = jax.nn.softmax(top_val, axis=-1)
    A = N * TOP_K
    flat_e = top_idx.reshape(A)
    flat_tok = jnp.arange(A, dtype=jnp.int32) // TOP_K
    order = jnp.argsort(flat_e)
    sorted_e = flat_e[order]
    counts = jnp.bincount(flat_e, length=N_EXPERTS)
    padded = (counts + MOE_BLOCK - 1) // MOE_BLOCK * MOE_BLOCK
    pad_end = jnp.cumsum(padded)
    pad_start = pad_end - padded
    grp_start = jnp.cumsum(counts) - counts
    dest = pad_start[sorted_e] + jnp.arange(A) - grp_start[sorted_e]
    n_blocks = (A + N_EXPERTS * (MOE_BLOCK - 1) + MOE_BLOCK - 1) // MOE_BLOCK
    P = n_blocks * MOE_BLOCK
    row_tok = jnp.zeros((P,), jnp.int32).at[dest].set(flat_tok[order])
    row_w = jnp.zeros((P,), jnp.float32).at[dest].set(probs.reshape(A)[order])
    block_e = jnp.minimum(jnp.searchsorted(pad_end, jnp.arange(n_blocks) * MOE_BLOCK, side='right'),
                          N_EXPERTS - 1)
    xs = h[row_tok].reshape(n_blocks, MOE_BLOCK, D)

    def expert_block(args):
        xb, e = args
        g = jnp.minimum(xb @ w_gate[e] + b_gate[e], SWIGLU_LIMIT)
        u = jnp.clip(xb @ w_up[e] + b_up[e], -SWIGLU_LIMIT, SWIGLU_LIMIT)
        act = g * jax.nn.sigmoid(SWIGLU_ALPHA * g) * (u + 1)
        return act @ w_down[e] + b_down[e]

    ys = lax.map(expert_block, (xs, block_e)).reshape(P, D)
    out = jax.ops.segment_sum(ys.astype(jnp.float32) * row_w[:, None], row_tok, num_segments=N)
    return out.astype(h.dtype)


def layer_forward(x, c, pos, w_ada, b_ada, norm1_g, w_in, conv_w, conv_b, conv_ln_g, conv_ln_b,
                  w_conv_out, q_norm_g, k_norm_g, w_attn_out, w_out, norm2_g, w_router, b_router,
                  w_gate, b_gate, w_up, b_up, w_down, b_down):
    B, S, D = x.shape
    mod = jax.nn.silu(c) @ w_ada + b_ada
    sh1, sc1, g1, sh2, sc2, g2 = [m[:, None, :] for m in jnp.split(mod, N_MOD, axis=-1)]

    h = rms_norm(x, norm1_g) * (1 + sc1) + sh1
    proj = h @ w_in
    a_conv, q, k, v, iq, ik, iw, gates = split_cols(proj)

    y_conv = conv_branch(a_conv, conv_w, conv_b, conv_ln_g, conv_ln_b, w_conv_out)

    q = rope(rms_norm(q.reshape(B, S, N_HEADS, HEAD_DIM), q_norm_g), pos)
    k = rope(rms_norm(k.reshape(B, S, N_KV_HEADS, HEAD_DIM), k_norm_g), pos)
    v = v.reshape(B, S, N_KV_HEADS, HEAD_DIM)
    iq = rope(iq.reshape(B, S, IDX_HEADS, IDX_DIM), pos)
    ik = rope(ik.reshape(B, S, 1, IDX_DIM), pos)[:, :, 0, :]
    iw = iw * (IDX_HEADS ** -0.5)
    y_attn = sparse_attention(q, k, v, iq, ik, iw) @ w_attn_out

    gates = jax.nn.sigmoid(gates.reshape(B, S, N_BRANCHES, D))
    mixed = gates[:, :, 0, :] * y_conv + gates[:, :, 1, :] * y_attn
    x = x + g1 * (mixed @ w_out)

    h2 = rms_norm(x, norm2_g) * (1 + sc2) + sh2
    y_moe = moe(h2.reshape(B * S, D), w_router, b_router, w_gate, b_gate, w_up, b_up,
                w_down, b_down).reshape(B, S, D)
    return x + g2 * y_moe


def setup_inputs(seed: int = 0) -> dict:
    key = jax.random.key(seed)
    ks = jax.random.split(key, 24)
    L = DEPTH

    def nrm(k, shape, scale):
        return jax.random.normal(k, shape, jnp.float32) * scale

    return {
        "x": nrm(ks[0], (BATCH, SEQ, D_MODEL), 1.0),
        "c": nrm(ks[1], (BATCH, D_MODEL), 1.0),
        "w_ada": nrm(ks[2], (L, D_MODEL, N_MOD * D_MODEL), 0.5 * D_MODEL ** -0.5),
        "b_ada": nrm(ks[3], (L, N_MOD * D_MODEL), 0.02),
        "norm1_g": 1.0 + nrm(ks[4], (L, D_MODEL), 0.02),
        "w_in": nrm(ks[5], (L, D_MODEL, D_IN), D_MODEL ** -0.5),
        "conv_w": nrm(ks[6], (L, CONV_TAPS, CONV_CHANNELS), CONV_TAPS ** -0.5),
        "conv_b": nrm(ks[7], (L, CONV_CHANNELS), 0.02),
        "conv_ln_g": 1.0 + nrm(ks[8], (L, CONV_CHANNELS), 0.02),
        "conv_ln_b": nrm(ks[9], (L, CONV_CHANNELS), 0.02),
        "w_conv_out": nrm(ks[10], (L, CONV_CHANNELS, D_MODEL), CONV_CHANNELS ** -0.5),
        "q_norm_g": 1.0 + nrm(ks[11], (L, HEAD_DIM), 0.02),
        "k_norm_g": 1.0 + nrm(ks[12], (L, HEAD_DIM), 0.02),
        "w_attn_out": nrm(ks[13], (L, ATTN_WIDTH, D_MODEL), ATTN_WIDTH ** -0.5),
        "w_out": nrm(ks[14], (L, D_MODEL, D_MODEL), D_MODEL ** -0.5),
        "norm2_g": 1.0 + nrm(ks[15], (L, D_MODEL), 0.02),
        "w_router": nrm(ks[16], (L, D_MODEL, N_EXPERTS), D_MODEL ** -0.5),
        "b_router": nrm(ks[17], (L, N_EXPERTS), 0.01),
        "w_gate": nrm(ks[18], (L, N_EXPERTS, D_MODEL, D_EXPERT), D_MODEL ** -0.5),
        "b_gate": nrm(ks[19], (L, N_EXPERTS, D_EXPERT), 0.02),
        "w_up": nrm(ks[20], (L, N_EXPERTS, D_MODEL, D_EXPERT), D_MODEL ** -0.5),
        "b_up": nrm(ks[21], (L, N_EXPERTS, D_EXPERT), 0.02),
        "w_down": nrm(ks[22], (L, N_EXPERTS, D_EXPERT, D_MODEL), D_EXPERT ** -0.5),
        "b_down": nrm(ks[23], (L, N_EXPERTS, D_MODEL), 0.02),
    }


def reference(x, c, w_ada, b_ada, norm1_g, w_in, conv_w, conv_b, conv_ln_g, conv_ln_b, w_conv_out,
              q_norm_g, k_norm_g, w_attn_out, w_out, norm2_g, w_router, b_router, w_gate, b_gate,
              w_up, b_up, w_down, b_down):
    pos = jnp.arange(x.shape[1])
    for l in range(DEPTH):
        x = layer_forward(x, c, pos, w_ada[l], b_ada[l], norm1_g[l], w_in[l], conv_w[l], conv_b[l],
                          conv_ln_g[l], conv_ln_b[l], w_conv_out[l], q_norm_g[l], k_norm_g[l],
                          w_attn_out[l], w_out[l], norm2_g[l], w_router[l], b_router[l], w_gate[l],
                          b_gate[l], w_up[l], b_up[l], w_down[l], b_down[l])
    return x
```

```python
import functools

import jax
import jax.numpy as jnp
from jax import lax
from jax.experimental import pallas as pl
from jax.experimental.pallas import tpu as pltpu

F32 = jnp.float32
BF16 = jnp.bfloat16
I32 = jnp.int32

EPS = 1e-6
CONV_TAPS = 31
N_HEADS = 8
N_KV_HEADS = 2
HEAD_DIM = 128
GROUP = N_HEADS // N_KV_HEADS
ATTN_WIDTH = N_HEADS * HEAD_DIM
ROPE_THETA = 10000.0
Q_BLOCK = 128
IDX_HEADS = 16
IDX_DIM = 64
IDX_WIDTH = IDX_HEADS * IDX_DIM
IDX_TOPK_MAX = 256
N_EXPERTS = 32
TOP_K = 4
SWIGLU_ALPHA = 1.702
SWIGLU_LIMIT = 7.0

LANES = 128
CONV_HALO = 32
KEY_CHUNK = 256
MOE_TILE = 1024
MOE_SUB = 256
MOE_FCHUNK = 256
NEG_BIG = -1e30
INT_MIN = -2 ** 31
VMEM_LIMIT = 56 * 1024 * 1024


def _tile(dim, pref):
    if dim <= pref:
        return dim
    t = pref - pref % LANES
    while t >= LANES:
        if dim % t == 0:
            return t
        t -= LANES
    return dim


def _params(sem):
    return pltpu.CompilerParams(dimension_semantics=sem, vmem_limit_bytes=VMEM_LIMIT)


def _ada_kernel(c_ref, w_ref, b_ref, o_ref):
    c = c_ref[...]
    sc = c * jax.nn.sigmoid(c)
    o_ref[...] = jnp.dot(sc.astype(BF16), w_ref[...].astype(BF16),
                         preferred_element_type=F32) + b_ref[...]


def _ada(c, w_ada, b_ada):
    B, D = c.shape
    n = w_ada.shape[1]
    rows = 8
    cp = jnp.zeros((rows, D), F32).at[:B].set(c)
    tn = _tile(n, 1024)
    out = pl.pallas_call(
        _ada_kernel,
        grid=(n // tn,),
        in_specs=[pl.BlockSpec((rows, D), lambda j: (0, 0)),
                  pl.BlockSpec((D, tn), lambda j: (0, j)),
                  pl.BlockSpec((1, tn), lambda j: (0, j))],
        out_specs=pl.BlockSpec((rows, tn), lambda j: (0, j)),
        out_shape=jax.ShapeDtypeStruct((rows, n), F32),
        compiler_params=_params(("arbitrary",)),
        name="ada_mod",
    )(cp, w_ada, b_ada.reshape(1, n))
    return out[:B]


def _normmod_kernel(x_ref, g_ref, sc_ref, sh_ref, o_ref):
    x = x_ref[0]
    ms = jnp.mean(x * x, axis=-1, keepdims=True)
    y = x * lax.rsqrt(ms + EPS) * g_ref[...]
    o_ref[0] = (y * (1.0 + sc_ref[0]) + sh_ref[0]).astype(o_ref.dtype)


def _normmod(x, g, sc, sh):
    B, S, D = x.shape
    ts = _tile(S, 512)
    return pl.pallas_call(
        _normmod_kernel,
        grid=(B, S // ts),
        in_specs=[pl.BlockSpec((1, ts, D), lambda b, i: (b, i, 0)),
                  pl.BlockSpec((1, D), lambda b, i: (0, 0)),
                  pl.BlockSpec((1, 1, D), lambda b, i: (b, 0, 0)),
                  pl.BlockSpec((1, 1, D), lambda b, i: (b, 0, 0))],
        out_specs=pl.BlockSpec((1, ts, D), lambda b, i: (b, i, 0)),
        out_shape=jax.ShapeDtypeStruct((B, S, D), BF16),
        compiler_params=_params(("arbitrary", "arbitrary")),
        name="norm1_mod",
    )(x, g.reshape(1, D), sc.reshape(B, 1, D), sh.reshape(B, 1, D))


def _rope_tables(S, d):
    inv = ROPE_THETA ** (-jnp.arange(0, d, 2, dtype=F32) / d)
    ang = jnp.arange(S, dtype=F32)[:, None] * inv[None, :]
    cos = jnp.cos(ang)
    sin = jnp.sin(ang)
    reps = LANES // d
    cos_t = jnp.tile(jnp.concatenate([cos, cos], axis=1), (1, reps))
    sin_t = jnp.tile(jnp.concatenate([-sin, sin], axis=1), (1, reps))
    return cos_t, sin_t


def _rope128(y, cos, sin):
    return y * cos + pltpu.roll(y, 64, axis=1) * sin


def _rope64(y, cos, sin):
    lane = lax.broadcasted_iota(I32, y.shape, 1)
    rot = jnp.where((lane % 64) < 32, pltpu.roll(y, 96, axis=1), pltpu.roll(y, 32, axis=1))
    return y * cos + rot * sin


def _glu_kernel(x_ref, w_ref, o_ref):
    acc = jnp.dot(x_ref[...], w_ref[...], preferred_element_type=F32)
    half = acc.shape[1] // 2
    o_ref[...] = (acc[:, :half] * jax.nn.sigmoid(acc[:, half:])).astype(o_ref.dtype)


def _q_kernel(x_ref, w_ref, cos_ref, sin_ref, g_ref, o_ref):
    acc = jnp.dot(x_ref[...], w_ref[...], preferred_element_type=F32)
    cos = cos_ref[...]
    sin = sin_ref[...]
    g = g_ref[...]
    for h in range(acc.shape[1] // HEAD_DIM):
        xh = acc[:, h * HEAD_DIM:(h + 1) * HEAD_DIM]
        ms = jnp.mean(xh * xh, axis=-1, keepdims=True)
        y = xh * lax.rsqrt(ms + EPS) * g
        o_ref[:, h * HEAD_DIM:(h + 1) * HEAD_DIM] = (
            _rope128(y, cos, sin) * (HEAD_DIM ** -0.5)).astype(o_ref.dtype)


def _kv_kernel(x_ref, w_ref, cos_ref, sin_ref, cos64_ref, sin64_ref, g_ref,
               k_ref, v_ref, ika_ref, ikb_ref, iw_ref):
    acc = jnp.dot(x_ref[...], w_ref[...], preferred_element_type=F32)
    kvw = N_KV_HEADS * HEAD_DIM
    cos = cos_ref[...]
    sin = sin_ref[...]
    g = g_ref[...]
    for h in range(N_KV_HEADS):
        xh = acc[:, h * HEAD_DIM:(h + 1) * HEAD_DIM]
        ms = jnp.mean(xh * xh, axis=-1, keepdims=True)
        y = xh * lax.rsqrt(ms + EPS) * g
        k_ref[:, h * HEAD_DIM:(h + 1) * HEAD_DIM] = _rope128(y, cos, sin).astype(k_ref.dtype)
    v_ref[...] = acc[:, kvw:2 * kvw].astype(v_ref.dtype)
    tail = acc[:, 2 * kvw:2 * kvw + LANES]
    lane = lax.broadcasted_iota(I32, tail.shape, 1)
    roped = _rope64(tail, cos64_ref[...], sin64_ref[...])
    ika = jnp.where(lane < IDX_DIM, roped, 0.0)
    ika_ref[...] = ika.astype(ika_ref.dtype)
    ikb_ref[...] = pltpu.roll(ika, IDX_DIM, axis=1).astype(ikb_ref.dtype)
    iw = pltpu.roll(tail, LANES - IDX_DIM, axis=1)
    iw_ref[...] = jnp.where(lane < IDX_HEADS, iw * (IDX_HEADS ** -0.5), 0.0)


def _iq_kernel(x_ref, w_ref, cos_ref, sin_ref, o_ref):
    acc = jnp.dot(x_ref[...], w_ref[...], preferred_element_type=F32)
    cos = cos_ref[...]
    sin = sin_ref[...]
    for p in range(acc.shape[1] // LANES):
        y = acc[:, p * LANES:(p + 1) * LANES]
        o_ref[:, p * LANES:(p + 1) * LANES] = (
            _rope64(y, cos, sin) * (IDX_DIM ** -0.5)).astype(o_ref.dtype)


def _sigmoid_kernel(x_ref, w_ref, o_ref):
    acc = jnp.dot(x_ref[...], w_ref[...], preferred_element_type=F32)
    o_ref[...] = jax.nn.sigmoid(acc).astype(o_ref.dtype)


def _proj(kernel, h, w, tn, aux, aux_specs, out_shapes, out_specs, tm, name):
    N, D = h.shape
    n = w.shape[1]
    return pl.pallas_call(
        kernel,
        grid=(N // tm, n // tn),
        in_specs=[pl.BlockSpec((tm, D), lambda i, j: (i, 0)),
                  pl.BlockSpec((D, tn), lambda i, j: (0, j))] + aux_specs,
        out_specs=out_specs,
        out_shape=out_shapes,
        compiler_params=_params(("arbitrary", "arbitrary")),
        name=name,
    )(h, w, *aux)


def _conv_kernel(u_ref, halo_ref, cw_ref, cb_ref, lg_ref, lb_ref, wo_ref, gate_ref, o_ref,
                 buf_ref, conv_ref):
    i = pl.program_id(1)
    ts = u_ref.shape[1]
    halo = halo_ref[0].astype(F32)
    buf_ref[0:CONV_HALO, :] = jnp.where(i == 0, 0.0, halo)
    buf_ref[CONV_HALO:, :] = u_ref[0].astype(F32)
    rc = 32
    first = CONV_HALO - (CONV_TAPS - 1)
    for c in range(ts // rc):
        acc = jnp.zeros((rc, buf_ref.shape[1]), F32)
        for t in range(CONV_TAPS):
            acc = acc + cw_ref[t:t + 1, :] * buf_ref[c * rc + first + t:c * rc + first + t + rc, :]
        conv_ref[c * rc:(c + 1) * rc, :] = acc
    u = conv_ref[...] + cb_ref[...]
    mu = jnp.mean(u, axis=-1, keepdims=True)
    uc = u - mu
    var = jnp.mean(uc * uc, axis=-1, keepdims=True)
    y = uc * lax.rsqrt(var + EPS) * lg_ref[...] + lb_ref[...]
    y = y * jax.nn.sigmoid(y)
    z = jnp.dot(y.astype(BF16), wo_ref[...], preferred_element_type=F32)
    o_ref[0] = (z * gate_ref[0].astype(F32)).astype(o_ref.dtype)


def _conv_branch(u, conv_w, conv_b, ln_g, ln_b, w_conv_out, gates):
    B, S, C = u.shape
    D = w_conv_out.shape[1]
    ts = _tile(S, 128)
    hb = ts // CONV_HALO
    cw = jnp.zeros((CONV_HALO, C), F32).at[:CONV_TAPS].set(conv_w)
    return pl.pallas_call(
        _conv_kernel,
        grid=(B, S // ts),
        in_specs=[pl.BlockSpec((1, ts, C), lambda b, i: (b, i, 0)),
                  pl.BlockSpec((1, CONV_HALO, C), lambda b, i: (b, jnp.maximum(i * hb - 1, 0), 0)),
                  pl.BlockSpec((CONV_HALO, C), lambda b, i: (0, 0)),
                  pl.BlockSpec((1, C), lambda b, i: (0, 0)),
                  pl.BlockSpec((1, C), lambda b, i: (0, 0)),
                  pl.BlockSpec((1, C), lambda b, i: (0, 0)),
                  pl.BlockSpec((C, D), lambda b, i: (0, 0)),
                  pl.BlockSpec((1, ts, D), lambda b, i: (b, i, 0))],
        out_specs=pl.BlockSpec((1, ts, D), lambda b, i: (b, i, 0)),
        out_shape=jax.ShapeDtypeStruct((B, S, D), BF16),
        scratch_shapes=[pltpu.VMEM((ts + CONV_HALO, C), F32), pltpu.VMEM((ts, C), F32)],
        compiler_params=_params(("arbitrary", "arbitrary")),
        name="conv_branch",
    )(u, u, cw, conv_b.reshape(1, C), ln_g.reshape(1, C), ln_b.reshape(1, C),
      w_conv_out.astype(BF16), gates)


def _attn_kernel(n_sel, iq_ref, iw_ref, q_ref, ika_ref, ikb_ref, k_ref, v_ref, o_ref,
                 key_ref, bias_ref, wb_ref, m_ref, l_ref, acc_ref):
    i = pl.program_id(1)
    QB = Q_BLOCK
    CK = KEY_CHUNK
    nck = (i * QB + QB + CK - 1) // CK
    NT = (((1,), (1,)), ((), ()))
    row = lax.broadcasted_iota(I32, (QB, LANES), 0)
    lane = lax.broadcasted_iota(I32, (QB, LANES), 1)
    q_pos = i * QB + row

    iw = iw_ref[...]
    for h in range(IDX_HEADS):
        wb_ref[h] = jnp.broadcast_to(iw[:, h:h + 1], (QB, LANES))

    def score_chunk(c, carry):
        k0 = pl.multiple_of(c * CK, CK)
        ka = ika_ref[pl.ds(k0, CK), :]
        kb = ikb_ref[pl.ds(k0, CK), :]
        acc = jnp.zeros((QB, CK), F32)
        for p in range(IDX_HEADS // 2):
            lhs = iq_ref[:, p * LANES:(p + 1) * LANES]
            sa = lax.dot_general(lhs, ka, NT, preferred_element_type=F32)
            sb = lax.dot_general(lhs, kb, NT, preferred_element_type=F32)
            wa = jnp.concatenate([wb_ref[2 * p]] * (CK // LANES), axis=1)
            wb = jnp.concatenate([wb_ref[2 * p + 1]] * (CK // LANES), axis=1)
            acc = acc + jnp.maximum(sa, 0.0) * wa + jnp.maximum(sb, 0.0) * wb
        bits = pltpu.bitcast(acc, I32)
        key = bits ^ ((bits >> 31) & 0x7FFFFFFF)
        for j in range(CK // LANES):
            kidx = k0 + j * LANES + lane
            key_ref[c, :, j * LANES:(j + 1) * LANES] = jnp.where(
                kidx <= q_pos, key[:, j * LANES:(j + 1) * LANES], INT_MIN)
        return carry

    lax.fori_loop(0, nck, score_chunk, 0)

    def count(pred):
        def body(c, cnt):
            for j in range(CK // LANES):
                kidx = c * CK + j * LANES + lane
                cnt = cnt + pred(key_ref[c, :, j * LANES:(j + 1) * LANES], kidx).astype(I32)
            return cnt
        cnt = lax.fori_loop(0, nck, body, jnp.zeros((QB, LANES), I32))
        return jnp.broadcast_to(jnp.sum(cnt, axis=1, keepdims=True), (QB, LANES))

    zero = jnp.zeros((QB, LANES), I32)
    thr = jnp.where(count(lambda key, kidx: key >= zero) >= n_sel, zero, INT_MIN)

    def bit_step(b, thr):
        cand = thr | (1 << (30 - b))
        return jnp.where(count(lambda key, kidx: key >= cand) >= n_sel, cand, thr)

    thr = lax.fori_loop(0, 31, bit_step, thr)

    n_gt = count(lambda key, kidx: key > thr)
    n_ge = count(lambda key, kidx: key >= thr)
    need = n_sel - n_gt
    bias_ref[0, :, 0:LANES] = jnp.full((QB, LANES), 2 ** 30, I32).astype(F32)

    @pl.when(jnp.max(jnp.where(thr > INT_MIN, n_ge, 0)) > n_sel)
    def _():
        def idx_step(b, bound):
            cand = bound | (1 << (14 - b))
            n_tie = count(lambda key, kidx: (key == thr) & (kidx < cand))
            return jnp.where(n_tie <= need, cand, bound)
        bound = lax.fori_loop(0, 15, idx_step, zero)
        bias_ref[0, :, 0:LANES] = bound.astype(F32)

    bound = bias_ref[0, :, 0:LANES].astype(I32)

    def bias_chunk(c, carry):
        for j in range(CK // LANES):
            kidx = c * CK + j * LANES + lane
            key = key_ref[c, :, j * LANES:(j + 1) * LANES]
            sel = ((key > thr) | ((key == thr) & (kidx < bound))) & (kidx <= q_pos)
            bias_ref[c, :, j * LANES:(j + 1) * LANES] = jnp.where(sel, 0.0, NEG_BIG)
        return carry

    lax.fori_loop(0, nck, bias_chunk, 0)

    for kh in range(N_KV_HEADS):
        qg = jnp.concatenate(
            [q_ref[:, (kh * GROUP + g) * HEAD_DIM:(kh * GROUP + g + 1) * HEAD_DIM]
             for g in range(GROUP)], axis=0)
        m_ref[...] = jnp.full(m_ref.shape, NEG_BIG, F32)
        l_ref[...] = jnp.zeros(l_ref.shape, F32)
        acc_ref[...] = jnp.zeros(acc_ref.shape, F32)

        def attn_chunk(c, carry):
            k0 = pl.multiple_of(c * CK, CK)
            kc = k_ref[pl.ds(k0, CK), kh * HEAD_DIM:(kh + 1) * HEAD_DIM]
            vc = v_ref[pl.ds(k0, CK), kh * HEAD_DIM:(kh + 1) * HEAD_DIM]
            s = lax.dot_general(qg, kc, NT, preferred_element_type=F32)
            s = s + jnp.concatenate([bias_ref[c]] * GROUP, axis=0)
            m_old = m_ref[...]
            m_new = jnp.maximum(m_old, jnp.max(s, axis=1, keepdims=True))
            alpha = jnp.exp(m_old - m_new)
            p = jnp.exp(s - m_new)
            l_ref[...] = alpha * l_ref[...] + jnp.sum(p, axis=1, keepdims=True)
            acc_ref[...] = alpha * acc_ref[...] + jnp.dot(p.astype(BF16), vc,
                                                          preferred_element_type=F32)
            m_ref[...] = m_new
            return carry

        lax.fori_loop(0, nck, attn_chunk, 0)
        o = acc_ref[...] / l_ref[...]
        for g in range(GROUP):
            h = kh * GROUP + g
            o_ref[:, h * HEAD_DIM:(h + 1) * HEAD_DIM] = o[g * QB:(g + 1) * QB].astype(o_ref.dtype)


def _sparse_attention(iq, iw, q, ika, ikb, k, v, B, S):
    N = B * S
    nb = S // Q_BLOCK
    n_sel = min(IDX_TOPK_MAX, S // 4)
    kvw = N_KV_HEADS * HEAD_DIM
    nchunks = S // KEY_CHUNK
    qspec = lambda w: pl.BlockSpec((Q_BLOCK, w), lambda b, i: (b * nb + i, 0))
    kspec = lambda w: pl.BlockSpec((S, w), lambda b, i: (b, 0))
    return pl.pallas_call(
        functools.partial(_attn_kernel, n_sel),
        grid=(B, nb),
        in_specs=[qspec(IDX_WIDTH), qspec(LANES), qspec(ATTN_WIDTH),
                  kspec(LANES), kspec(LANES), kspec(kvw), kspec(kvw)],
        out_specs=qspec(ATTN_WIDTH),
        out_shape=jax.ShapeDtypeStruct((N, ATTN_WIDTH), BF16),
        scratch_shapes=[pltpu.VMEM((nchunks, Q_BLOCK, KEY_CHUNK), I32),
                        pltpu.VMEM((nchunks, Q_BLOCK, KEY_CHUNK), F32),
                        pltpu.VMEM((IDX_HEADS, Q_BLOCK, LANES), F32),
                        pltpu.VMEM((GROUP * Q_BLOCK, 1), F32),
                        pltpu.VMEM((GROUP * Q_BLOCK, 1), F32),
                        pltpu.VMEM((GROUP * Q_BLOCK, HEAD_DIM), F32)],
        compiler_params=_params(("arbitrary", "arbitrary")),
        name="dsa_attention",
    )(iq, iw, q, ika, ikb, k, v)


def _split_bf16(a):
    hi = a.astype(BF16)
    lo = (a - hi.astype(F32)).astype(BF16)
    return hi, lo


def _mix_kernel(yc_ref, ao_ref, gate_ref, x_ref, g1_ref, wao_ref, wo_ref, n2_ref, sc2_ref, sh2_ref,
                wrh_ref, wrl_ref, br_ref, x1_ref, h2_ref, idx_ref, prob_ref):
    ya = jnp.dot(ao_ref[...], wao_ref[...], preferred_element_type=F32)
    mixed = yc_ref[...].astype(F32) + gate_ref[...].astype(F32) * ya
    z = jnp.dot(mixed.astype(BF16), wo_ref[...], preferred_element_type=F32)
    x1 = x_ref[...] + g1_ref[0] * z
    x1_ref[...] = x1
    ms = jnp.mean(x1 * x1, axis=-1, keepdims=True)
    h2 = x1 * lax.rsqrt(ms + EPS) * n2_ref[...] * (1.0 + sc2_ref[0]) + sh2_ref[0]
    h2_ref[...] = h2.astype(h2_ref.dtype)
    hh, hl = _split_bf16(h2)
    logits = (jnp.dot(hh, wrh_ref[...], preferred_element_type=F32)
              + jnp.dot(hh, wrl_ref[...], preferred_element_type=F32)
              + jnp.dot(hl, wrh_ref[...], preferred_element_type=F32)) + br_ref[...]
    lane = lax.broadcasted_iota(I32, logits.shape, 1)
    work = jnp.where(lane < N_EXPERTS, logits, -jnp.inf)
    vals = []
    idx_out = jnp.zeros(logits.shape, I32)
    for k in range(TOP_K):
        mv = jnp.max(work, axis=-1, keepdims=True)
        mi = jnp.min(jnp.where(work == mv, lane, LANES), axis=-1, keepdims=True)
        idx_out = jnp.where(lane == k, mi, idx_out)
        work = jnp.where(lane == mi, -jnp.inf, work)
        vals.append(mv)
    es = [jnp.exp(vk - vals[0]) for vk in vals]
    den = es[0]
    for e in es[1:]:
        den = den + e
    prob_out = jnp.zeros(logits.shape, F32)
    for k in range(TOP_K):
        prob_out = jnp.where(lane == k, es[k] / den, prob_out)
    idx_ref[...] = idx_out
    prob_ref[...] = prob_out


def _mix(yc, ao, gates, x, g1, w_attn_out, w_out, norm2_g, sc2, sh2, w_router, b_router, B, S):
    N, D = x.shape
    tm = _tile(S, 512)
    nS = S // tm
    wr = jnp.zeros((D, LANES), F32).at[:, :N_EXPERTS].set(w_router)
    wrh, wrl = _split_bf16(wr)
    br = jnp.zeros((1, LANES), F32).at[0, :N_EXPERTS].set(b_router)
    row = lambda w, c=0: pl.BlockSpec((tm, w), lambda i: (i, c))
    full = lambda a: pl.BlockSpec(a.shape, lambda i: (0,) * a.ndim)
    per_b = pl.BlockSpec((1, 1, D), lambda i: (i // nS, 0, 0))
    wao = w_attn_out.astype(BF16)
    wo = w_out.astype(BF16)
    n2 = norm2_g.reshape(1, D)
    return pl.pallas_call(
        _mix_kernel,
        grid=(N // tm,),
        in_specs=[row(D), row(ATTN_WIDTH), row(D, 1), row(D), per_b, full(wao), full(wo),
                  full(n2), per_b, per_b, full(wrh), full(wrl), full(br)],
        out_specs=[row(D), row(D), row(LANES), row(LANES)],
        out_shape=[jax.ShapeDtypeStruct((N, D), F32), jax.ShapeDtypeStruct((N, D), BF16),
                   jax.ShapeDtypeStruct((N, LANES), I32), jax.ShapeDtypeStruct((N, LANES), F32)],
        compiler_params=_params(("arbitrary",)),
        name="mix_router",
    )(yc, ao, gates, x, g1.reshape(B, 1, D), wao, wo, n2, sc2.reshape(B, 1, D),
      sh2.reshape(B, 1, D), wrh, wrl, br)


def _moe_kernel(blk_ref, exp_ref, nsub_ref, x_ref, rw_ref, wg_ref, bg_ref, wu_ref, bu_ref,
                wd_ref, bd_ref, o_ref, wg_s, wu_s, wd_s, acc_ref):
    w = pl.program_id(0)
    f = pl.program_id(1)
    nf = pl.num_programs(1)
    nsub = nsub_ref[w]

    @pl.when(nsub > 0)
    def _():
        wg_s[...] = wg_ref[0].astype(BF16)
        wu_s[...] = wu_ref[0].astype(BF16)
        wd_s[...] = wd_ref[0].astype(BF16)

        @pl.when(f == 0)
        def _():
            def zero(s, carry):
                r0 = pl.multiple_of(s * MOE_SUB, MOE_SUB)
                acc_ref[pl.ds(r0, MOE_SUB), :] = jnp.zeros((MOE_SUB, acc_ref.shape[1]), F32)
                return carry
            lax.fori_loop(0, nsub, zero, 0)

        def body(s, carry):
            r0 = pl.multiple_of(s * MOE_SUB, MOE_SUB)
            xs = x_ref[pl.ds(r0, MOE_SUB), :]
            g = jnp.dot(xs, wg_s[...], preferred_element_type=F32) + bg_ref[0]
            u = jnp.dot(xs, wu_s[...], preferred_element_type=F32) + bu_ref[0]
            g = jnp.minimum(g, SWIGLU_LIMIT)
            u = jnp.clip(u, -SWIGLU_LIMIT, SWIGLU_LIMIT)
            act = g * jax.nn.sigmoid(SWIGLU_ALPHA * g) * (u + 1.0)
            acc_ref[pl.ds(r0, MOE_SUB), :] += jnp.dot(act.astype(BF16), wd_s[...],
                                                      preferred_element_type=F32)
            return carry
        lax.fori_loop(0, nsub, body, 0)

        @pl.when(f == nf - 1)
        def _():
            def fin(s, carry):
                r0 = pl.multiple_of(s * MOE_SUB, MOE_SUB)
                y = (acc_ref[pl.ds(r0, MOE_SUB), :] + bd_ref[0]) * rw_ref[pl.ds(r0, MOE_SUB), :]
                o_ref[pl.ds(r0, MOE_SUB), :] = y.astype(o_ref.dtype)
                return carry
            lax.fori_loop(0, nsub, fin, 0)


def _moe_experts(xs, row_w, item_blk, item_exp, item_nsub, w_gate, b_gate, w_up, b_up,
                 w_down, b_down):
    P, D = xs.shape
    E, _, F = w_gate.shape
    n_items = item_blk.shape[0]
    tf = _tile(F, MOE_FCHUNK)
    nf = F // tf

    def fidx(f, ns):
        return jnp.where(ns > 0, f, nf - 1)

    grid_spec = pltpu.PrefetchScalarGridSpec(
        num_scalar_prefetch=3,
        grid=(n_items, nf),
        in_specs=[
            pl.BlockSpec((MOE_TILE, D), lambda w, f, blk, ex, ns: (blk[w], 0)),
            pl.BlockSpec((MOE_TILE, 1), lambda w, f, blk, ex, ns: (blk[w], 0)),
            pl.BlockSpec((1, D, tf), lambda w, f, blk, ex, ns: (ex[w], 0, fidx(f, ns[w]))),
            pl.BlockSpec((1, 1, tf), lambda w, f, blk, ex, ns: (ex[w], 0, fidx(f, ns[w]))),
            pl.BlockSpec((1, D, tf), lambda w, f, blk, ex, ns: (ex[w], 0, fidx(f, ns[w]))),
            pl.BlockSpec((1, 1, tf), lambda w, f, blk, ex, ns: (ex[w], 0, fidx(f, ns[w]))),
            pl.BlockSpec((1, tf, D), lambda w, f, blk, ex, ns: (ex[w], fidx(f, ns[w]), 0)),
            pl.BlockSpec((1, 1, D), lambda w, f, blk, ex, ns: (ex[w], 0, 0)),
        ],
        out_specs=pl.BlockSpec((MOE_TILE, D), lambda w, f, blk, ex, ns: (blk[w], 0)),
        scratch_shapes=[pltpu.VMEM((D, tf), BF16), pltpu.VMEM((D, tf), BF16),
                        pltpu.VMEM((tf, D), BF16), pltpu.VMEM((MOE_TILE, D), F32)],
    )
    return pl.pallas_call(
        _moe_kernel,
        grid_spec=grid_spec,
        out_shape=jax.ShapeDtypeStruct((P, D), BF16),
        compiler_params=_params(("arbitrary", "arbitrary")),
        name="moe_experts",
    )(item_blk, item_exp, item_nsub, xs, row_w, w_gate, b_gate.reshape(E, 1, F), w_up,
      b_up.reshape(E, 1, F), w_down, b_down.reshape(E, 1, D))


def _route(top_idx, probs):
    N = top_idx.shape[0]
    A = N * TOP_K
    n_items = A // MOE_TILE + N_EXPERTS
    P = n_items * MOE_TILE
    flat_e = top_idx.reshape(A)
    onehot = (flat_e[:, None] == jnp.arange(N_EXPERTS, dtype=I32)[None, :]).astype(I32)
    csum = jnp.cumsum(onehot, axis=0)
    counts = csum[-1]
    rank = jnp.sum(onehot * csum, axis=1) - 1
    padded = (counts + MOE_TILE - 1) // MOE_TILE * MOE_TILE
    pad_end = jnp.cumsum(padded)
    pad_start = pad_end - padded
    grp_start = jnp.cumsum(counts) - counts
    dest = pad_start[flat_e] + rank
    order = jnp.argsort(flat_e)
    rows = jnp.arange(P, dtype=I32)
    row_e = jnp.minimum(jnp.searchsorted(pad_end, rows, side='right'), N_EXPERTS - 1).astype(I32)
    off = rows - pad_start[row_e]
    valid = off < counts[row_e]
    src = order[jnp.clip(grp_start[row_e] + off, 0, A - 1)]
    row_tok = jnp.where(valid, src // TOP_K, 0).astype(I32)
    row_w = jnp.where(valid, probs.reshape(A)[src], 0.0).astype(F32)
    items = jnp.arange(n_items, dtype=I32)
    n_used = (pad_end[-1] // MOE_TILE).astype(I32)
    last = jnp.maximum(n_used - 1, 0)
    item_blk = jnp.minimum(items, last)
    item_exp = row_e[item_blk * MOE_TILE]
    filled = jnp.clip(counts[item_exp] - (item_blk * MOE_TILE - pad_start[item_exp]), 0, MOE_TILE)
    item_nsub = jnp.where(items < n_used, (filled + MOE_SUB - 1) // MOE_SUB, 0).astype(I32)
    return row_tok, row_w, dest.astype(I32), item_blk, item_exp, item_nsub


def _final_kernel(x1_ref, y_ref, g2_ref, o_ref):
    y = y_ref[0].astype(F32)
    for k in range(1, TOP_K):
        y = y + y_ref[k].astype(F32)
    o_ref[...] = x1_ref[...] + g2_ref[0] * y


def _final(x1, yk, g2, B, S):
    N, D = x1.shape
    tm = _tile(S, 512)
    nS = S // tm
    return pl.pallas_call(
        _final_kernel,
        grid=(N // tm,),
        in_specs=[pl.BlockSpec((tm, D), lambda i: (i, 0)),
                  pl.BlockSpec((TOP_K, tm, D), lambda i: (0, i, 0)),
                  pl.BlockSpec((1, 1, D), lambda i: (i // nS, 0, 0))],
        out_specs=pl.BlockSpec((tm, D), lambda i: (i, 0)),
        out_shape=jax.ShapeDtypeStruct((N, D), F32),
        compiler_params=_params(("arbitrary",)),
        name="moe_combine",
    )(x1, yk, g2.reshape(B, 1, D))


def _layer(x, c, w_ada, b_ada, norm1_g, w_in, conv_w, conv_b, conv_ln_g, conv_ln_b, w_conv_out,
           q_norm_g, k_norm_g, w_attn_out, w_out, norm2_g, w_router, b_router, w_gate, b_gate,
           w_up, b_up, w_down, b_down):
    B, S, D = x.shape
    N = B * S
    C = conv_w.shape[1]
    kvw = N_KV_HEADS * HEAD_DIM

    mod = _ada(c, w_ada, b_ada)
    sh1, sc1, g1, sh2, sc2, g2 = [mod[:, j * D:(j + 1) * D] for j in range(6)]

    h = _normmod(x, norm1_g, sc1, sh1).reshape(N, D)

    o = 0
    w_conv_in = w_in[:, o:o + 2 * C]; o += 2 * C
    w_q = w_in[:, o:o + ATTN_WIDTH]; o += ATTN_WIDTH
    w_kv_small = w_in[:, o:o + 2 * kvw + IDX_WIDTH + IDX_DIM + IDX_HEADS]
    w_k_v = w_in[:, o:o + 2 * kvw]; o += 2 * kvw
    w_iq = w_in[:, o:o + IDX_WIDTH]; o += IDX_WIDTH
    w_ik_iw = w_in[:, o:o + IDX_DIM + IDX_HEADS]; o += IDX_DIM + IDX_HEADS
    w_gates = w_in[:, o:o + 2 * D]
    del w_kv_small

    tm = _tile(S, 1024)
    nS = S // tm
    cos128, sin128 = _rope_tables(S, HEAD_DIM)
    cos64, sin64 = _rope_tables(S, IDX_DIM)
    tab = pl.BlockSpec((tm, LANES), lambda i, j: (i % nS, 0))
    vec = pl.BlockSpec((1, LANES), lambda i, j: (0, 0))

    tn = _tile(2 * C, 1024)
    hc = tn // 2
    wv = w_conv_in[:, :C].reshape(D, C // hc, hc)
    wg = w_conv_in[:, C:].reshape(D, C // hc, hc)
    w_glu = jnp.concatenate([wv, wg], axis=2).reshape(D, 2 * C).astype(BF16)
    u = _proj(_glu_kernel, h, w_glu, tn, [], [],
              jax.ShapeDtypeStruct((N, C), BF16),
              pl.BlockSpec((tm, hc), lambda i, j: (i, j)), tm, "proj_conv_glu")

    tq = _tile(ATTN_WIDTH, 512)
    q = _proj(_q_kernel, h, w_q.astype(BF16), tq, [cos128, sin128, q_norm_g.reshape(1, HEAD_DIM)],
              [tab, tab, vec], jax.ShapeDtypeStruct((N, ATTN_WIDTH), BF16),
              pl.BlockSpec((tm, tq), lambda i, j: (i, j)), tm, "proj_q")

    pad = LANES - IDX_DIM - IDX_HEADS
    w_kvs = jnp.concatenate([w_k_v, w_ik_iw, jnp.zeros((D, pad), F32)], axis=1).astype(BF16)
    nk = w_kvs.shape[1]
    blk = lambda w: pl.BlockSpec((tm, w), lambda i, j: (i, 0))
    k, v, ika, ikb, iw = _proj(
        _kv_kernel, h, w_kvs, nk,
        [cos128, sin128, cos64, sin64, k_norm_g.reshape(1, HEAD_DIM)], [tab, tab, tab, tab, vec],
        [jax.ShapeDtypeStruct((N, kvw), BF16), jax.ShapeDtypeStruct((N, kvw), BF16),
         jax.ShapeDtypeStruct((N, LANES), BF16), jax.ShapeDtypeStruct((N, LANES), BF16),
         jax.ShapeDtypeStruct((N, LANES), F32)],
        [blk(kvw), blk(kvw), blk(LANES), blk(LANES), blk(LANES)], tm, "proj_kv_indexer_key")

    ti = _tile(IDX_WIDTH, 512)
    iq = _proj(_iq_kernel, h, w_iq.astype(BF16), ti, [cos64, sin64], [tab, tab],
               jax.ShapeDtypeStruct((N, IDX_WIDTH), BF16),
               pl.BlockSpec((tm, ti), lambda i, j: (i, j)), tm, "proj_iq")

    tg = _tile(2 * D, 1024)
    gates = _proj(_sigmoid_kernel, h, w_gates.astype(BF16), tg, [], [],
                  jax.ShapeDtypeStruct((N, 2 * D), BF16),
                  pl.BlockSpec((tm, tg), lambda i, j: (i, j)), tm, "proj_gates")

    yc = _conv_branch(u.reshape(B, S, C), conv_w, conv_b, conv_ln_g, conv_ln_b, w_conv_out,
                      gates.reshape(B, S, 2 * D)).reshape(N, D)
    ao = _sparse_attention(iq, iw, q, ika, ikb, k, v, B, S)

    x1, h2, top_idx, probs = _mix(yc, ao, gates, x.reshape(N, D), g1, w_attn_out, w_out, norm2_g,
                                  sc2, sh2, w_router, b_router, B, S)

    row_tok, row_w, dest, item_blk, item_exp, item_nsub = _route(top_idx[:, :TOP_K],
                                                                 probs[:, :TOP_K])
    xs = jnp.take(h2, row_tok, axis=0)
    ys = _moe_experts(xs, row_w[:, None], item_blk, item_exp, item_nsub, w_gate, b_gate, w_up,
                      b_up, w_down, b_down)
    yk = jnp.take(ys, dest.reshape(N, TOP_K).T, axis=0)
    out = _final(x1, yk, g2, B, S)
    return out.reshape(B, S, D)


def kernel(x, c, w_ada, b_ada, norm1_g, w_in, conv_w, conv_b, conv_ln_g, conv_ln_b, w_conv_out,
           q_norm_g, k_norm_g, w_attn_out, w_out, norm2_g, w_router, b_router, w_gate, b_gate,
           w_up, b_up, w_down, b_down):
    for l in range(w_ada.shape[0]):
        x = _layer(x, c, w_ada[l], b_ada[l], norm1_g[l], w_in[l], conv_w[l], conv_b[l],
                   conv_ln_g[l], conv_ln_b[l], w_conv_out[l], q_norm_g[l], k_norm_g[l],
                   w_attn_out[l], w_out[l], norm2_g[l], w_router[l], b_router[l], w_gate[l],
                   b_gate[l], w_up[l], b_up[l], w_down[l], b_down[l])
    return x
```

```python
import functools

import jax
import jax.numpy as jnp
from jax import lax
from jax.experimental import pallas as pl
from jax.experimental.pallas import tpu as pltpu

F32 = jnp.float32
BF16 = jnp.bfloat16
I32 = jnp.int32

EPS = 1e-6
CONV_TAPS = 31
N_HEADS = 8
N_KV_HEADS = 2
HEAD_DIM = 128
GROUP = N_HEADS // N_KV_HEADS
ATTN_WIDTH = N_HEADS * HEAD_DIM
ROPE_THETA = 10000.0
Q_BLOCK = 128
IDX_HEADS = 16
IDX_DIM = 64
IDX_WIDTH = IDX_HEADS * IDX_DIM
IDX_TOPK_MAX = 256
N_EXPERTS = 32
TOP_K = 4
SWIGLU_ALPHA = 1.702
SWIGLU_LIMIT = 7.0

LANES = 128
CONV_HALO = 32
KEY_CHUNK = 256
MOE_TILE = 1024
MOE_SUB = 256
MOE_FCHUNK = 256
NEG_BIG = -1e30
INT_MIN = -2 ** 31
VMEM_LIMIT = 56 * 1024 * 1024


def _tile(dim, pref):
    if dim <= pref:
        return dim
    t = pref - pref % LANES
    while t >= LANES:
        if dim % t == 0:
            return t
        t -= LANES
    return dim


def _params(sem):
    return pltpu.CompilerParams(dimension_semantics=sem, vmem_limit_bytes=VMEM_LIMIT)


def _ada_kernel(c_ref, w_ref, b_ref, o_ref):
    c = c_ref[...]
    sc = c * jax.nn.sigmoid(c)
    o_ref[...] = jnp.dot(sc.astype(BF16), w_ref[...].astype(BF16),
                         preferred_element_type=F32) + b_ref[...]


def _ada(c, w_ada, b_ada):
    B, D = c.shape
    n = w_ada.shape[1]
    rows = 8
    cp = jnp.zeros((rows, D), F32).at[:B].set(c)
    tn = _tile(n, 1024)
    out = pl.pallas_call(
        _ada_kernel,
        grid=(n // tn,),
        in_specs=[pl.BlockSpec((rows, D), lambda j: (0, 0)),
                  pl.BlockSpec((D, tn), lambda j: (0, j)),
                  pl.BlockSpec((1, tn), lambda j: (0, j))],
        out_specs=pl.BlockSpec((rows, tn), lambda j: (0, j)),
        out_shape=jax.ShapeDtypeStruct((rows, n), F32),
        compiler_params=_params(("arbitrary",)),
        name="ada_mod",
    )(cp, w_ada, b_ada.reshape(1, n))
    return out[:B]


def _normmod_kernel(x_ref, g_ref, sc_ref, sh_ref, o_ref):
    x = x_ref[0]
    ms = jnp.mean(x * x, axis=-1, keepdims=True)
    y = x * lax.rsqrt(ms + EPS) * g_ref[...]
    o_ref[0] = (y * (1.0 + sc_ref[0]) + sh_ref[0]).astype(o_ref.dtype)


def _normmod(x, g, sc, sh):
    B, S, D = x.shape
    ts = _tile(S, 512)
    return pl.pallas_call(
        _normmod_kernel,
        grid=(B, S // ts),
        in_specs=[pl.BlockSpec((1, ts, D), lambda b, i: (b, i, 0)),
                  pl.BlockSpec((1, D), lambda b, i: (0, 0)),
                  pl.BlockSpec((1, 1, D), lambda b, i: (b, 0, 0)),
                  pl.BlockSpec((1, 1, D), lambda b, i: (b, 0, 0))],
        out_specs=pl.BlockSpec((1, ts, D), lambda b, i: (b, i, 0)),
        out_shape=jax.ShapeDtypeStruct((B, S, D), BF16),
        compiler_params=_params(("arbitrary", "arbitrary")),
        name="norm1_mod",
    )(x, g.reshape(1, D), sc.reshape(B, 1, D), sh.reshape(B, 1, D))


def _rope_tables(S, d):
    inv = ROPE_THETA ** (-jnp.arange(0, d, 2, dtype=F32) / d)
    ang = jnp.arange(S, dtype=F32)[:, None] * inv[None, :]
    cos = jnp.cos(ang)
    sin = jnp.sin(ang)
    reps = LANES // d
    cos_t = jnp.tile(jnp.concatenate([cos, cos], axis=1), (1, reps))
    sin_t = jnp.tile(jnp.concatenate([-sin, sin], axis=1), (1, reps))
    return cos_t, sin_t


def _rope128(y, cos, sin):
    return y * cos + pltpu.roll(y, 64, axis=1) * sin


def _rope64(y, cos, sin):
    lane = lax.broadcasted_iota(I32, y.shape, 1)
    rot = jnp.where((lane % 64) < 32, pltpu.roll(y, 96, axis=1), pltpu.roll(y, 32, axis=1))
    return y * cos + rot * sin


def _glu_kernel(x_ref, w_ref, o_ref):
    acc = jnp.dot(x_ref[...], w_ref[...], preferred_element_type=F32)
    half = acc.shape[1] // 2
    o_ref[...] = (acc[:, :half] * jax.nn.sigmoid(acc[:, half:])).astype(o_ref.dtype)


def _q_kernel(x_ref, w_ref, cos_ref, sin_ref, g_ref, o_ref):
    acc = jnp.dot(x_ref[...], w_ref[...], preferred_element_type=F32)
    cos = cos_ref[...]
    sin = sin_ref[...]
    g = g_ref[...]
    for h in range(acc.shape[1] // HEAD_DIM):
        xh = acc[:, h * HEAD_DIM:(h + 1) * HEAD_DIM]
        ms = jnp.mean(xh * xh, axis=-1, keepdims=True)
        y = xh * lax.rsqrt(ms + EPS) * g
        o_ref[:, h * HEAD_DIM:(h + 1) * HEAD_DIM] = (
            _rope128(y, cos, sin) * (HEAD_DIM ** -0.5)).astype(o_ref.dtype)


def _kv_kernel(x_ref, w_ref, cos_ref, sin_ref, cos64_ref, sin64_ref, g_ref,
               k_ref, vt_ref, ika_ref, iw_ref):
    acc = jnp.dot(x_ref[...], w_ref[...], preferred_element_type=F32)
    kvw = N_KV_HEADS * HEAD_DIM
    cos = cos_ref[...]
    sin = sin_ref[...]
    g = g_ref[...]
    for h in range(N_KV_HEADS):
        xh = acc[:, h * HEAD_DIM:(h + 1) * HEAD_DIM]
        ms = jnp.mean(xh * xh, axis=-1, keepdims=True)
        y = xh * lax.rsqrt(ms + EPS) * g
        k_ref[:, h * HEAD_DIM:(h + 1) * HEAD_DIM] = _rope128(y, cos, sin).astype(k_ref.dtype)
    vt = acc[:, kvw:2 * kvw].T.astype(vt_ref.dtype)
    for c in range(vt_ref.shape[0]):
        vt_ref[c] = vt[:, c * KEY_CHUNK:(c + 1) * KEY_CHUNK]
    tail = acc[:, 2 * kvw:2 * kvw + LANES]
    lane = lax.broadcasted_iota(I32, tail.shape, 1)
    roped = _rope64(tail, cos64_ref[...], sin64_ref[...])
    ika_ref[...] = jnp.where(lane < IDX_DIM, roped, 0.0).astype(ika_ref.dtype)
    iw = pltpu.roll(tail, LANES - IDX_DIM, axis=1)
    iw_ref[...] = jnp.where(lane < IDX_HEADS, iw * (IDX_HEADS ** -0.5), 0.0)


def _iq_kernel(x_ref, w_ref, cos_ref, sin_ref, o_ref):
    acc = jnp.dot(x_ref[...], w_ref[...], preferred_element_type=F32)
    cos = cos_ref[...]
    sin = sin_ref[...]
    for p in range(acc.shape[1] // LANES):
        y = acc[:, p * LANES:(p + 1) * LANES]
        o_ref[:, p * LANES:(p + 1) * LANES] = (
            _rope64(y, cos, sin) * (IDX_DIM ** -0.5)).astype(o_ref.dtype)


def _sigmoid_kernel(x_ref, w_ref, o_ref):
    acc = jnp.dot(x_ref[...], w_ref[...], preferred_element_type=F32)
    o_ref[...] = jax.nn.sigmoid(acc).astype(o_ref.dtype)


def _proj(kernel, h, w, tn, aux, aux_specs, out_shapes, out_specs, tm, name):
    N, D = h.shape
    n = w.shape[1]
    return pl.pallas_call(
        kernel,
        grid=(N // tm, n // tn),
        in_specs=[pl.BlockSpec((tm, D), lambda i, j: (i, 0)),
                  pl.BlockSpec((D, tn), lambda i, j: (0, j))] + aux_specs,
        out_specs=out_specs,
        out_shape=out_shapes,
        compiler_params=_params(("arbitrary", "arbitrary")),
        name=name,
    )(h, w, *aux)


def _conv_kernel(u_ref, halo_ref, cw_ref, cb_ref, lg_ref, lb_ref, wo_ref, gate_ref, o_ref,
                 buf_ref, conv_ref):
    i = pl.program_id(1)
    ts = u_ref.shape[1]
    halo = halo_ref[0].astype(F32)
    buf_ref[0:CONV_HALO, :] = jnp.where(i == 0, 0.0, halo)
    buf_ref[CONV_HALO:, :] = u_ref[0].astype(F32)
    rc = 32
    first = CONV_HALO - (CONV_TAPS - 1)
    for c in range(ts // rc):
        acc = jnp.zeros((rc, buf_ref.shape[1]), F32)
        for t in range(CONV_TAPS):
            acc = acc + cw_ref[t:t + 1, :] * buf_ref[c * rc + first + t:c * rc + first + t + rc, :]
        conv_ref[c * rc:(c + 1) * rc, :] = acc
    u = conv_ref[...] + cb_ref[...]
    mu = jnp.mean(u, axis=-1, keepdims=True)
    uc = u - mu
    var = jnp.mean(uc * uc, axis=-1, keepdims=True)
    y = uc * lax.rsqrt(var + EPS) * lg_ref[...] + lb_ref[...]
    y = y * jax.nn.sigmoid(y)
    z = jnp.dot(y.astype(BF16), wo_ref[...], preferred_element_type=F32)
    o_ref[0] = (z * gate_ref[0].astype(F32)).astype(o_ref.dtype)


def _conv_branch(u, conv_w, conv_b, ln_g, ln_b, w_conv_out, gates):
    B, S, C = u.shape
    D = w_conv_out.shape[1]
    ts = _tile(S, 128)
    hb = ts // CONV_HALO
    cw = jnp.zeros((CONV_HALO, C), F32).at[:CONV_TAPS].set(conv_w)
    return pl.pallas_call(
        _conv_kernel,
        grid=(B, S // ts),
        in_specs=[pl.BlockSpec((1, ts, C), lambda b, i: (b, i, 0)),
                  pl.BlockSpec((1, CONV_HALO, C), lambda b, i: (b, jnp.maximum(i * hb - 1, 0), 0)),
                  pl.BlockSpec((CONV_HALO, C), lambda b, i: (0, 0)),
                  pl.BlockSpec((1, C), lambda b, i: (0, 0)),
                  pl.BlockSpec((1, C), lambda b, i: (0, 0)),
                  pl.BlockSpec((1, C), lambda b, i: (0, 0)),
                  pl.BlockSpec((C, D), lambda b, i: (0, 0)),
                  pl.BlockSpec((1, ts, D), lambda b, i: (b, i, 0))],
        out_specs=pl.BlockSpec((1, ts, D), lambda b, i: (b, i, 0)),
        out_shape=jax.ShapeDtypeStruct((B, S, D), BF16),
        scratch_shapes=[pltpu.VMEM((ts + CONV_HALO, C), F32), pltpu.VMEM((ts, C), F32)],
        compiler_params=_params(("arbitrary", "arbitrary")),
        name="conv_branch",
    )(u, u, cw, conv_b.reshape(1, C), ln_g.reshape(1, C), ln_b.reshape(1, C),
      w_conv_out.astype(BF16), gates)


def _attn_kernel(n_sel, iq_ref, iw_ref, q_ref, ika_ref, k_ref, vt_ref, o_ref,
                 key_ref, bias_ref, iqt_ref, wt_ref, qt_ref, bound_ref, m_ref, l_ref, acc_ref):
    i = pl.program_id(1)
    QB = Q_BLOCK
    CK = KEY_CHUNK
    GW = GROUP * QB
    SUBL = 8
    nck = (i * QB + QB + CK - 1) // CK
    q_pos = i * QB + lax.broadcasted_iota(I32, (CK, QB), 1)
    krow = lax.broadcasted_iota(I32, (CK, QB), 0)
    q_pos8 = q_pos[0:SUBL]
    row8 = krow[0:SUBL]

    iq = iq_ref[...].astype(F32)
    iqt_ref[IDX_DIM:, :] = jnp.zeros((LANES - IDX_DIM, IDX_HEADS * QB), BF16)
    for p in range(IDX_HEADS // 2):
        t = iq[:, p * LANES:(p + 1) * LANES].T
        iqt_ref[0:IDX_DIM, (2 * p) * QB:(2 * p + 1) * QB] = t[0:IDX_DIM].astype(BF16)
        iqt_ref[0:IDX_DIM, (2 * p + 1) * QB:(2 * p + 2) * QB] = t[IDX_DIM:].astype(BF16)
    wt_ref[...] = iw_ref[...].T
    qf = q_ref[...].astype(F32)
    for h in range(N_HEADS):
        qt_ref[:, h * QB:(h + 1) * QB] = qf[:, h * HEAD_DIM:(h + 1) * HEAD_DIM].T.astype(BF16)

    def score_chunk(c, carry):
        k0 = pl.multiple_of(c * CK, CK)
        ka = ika_ref[pl.ds(k0, CK), :]
        acc = jnp.zeros((CK, QB), F32)
        for hg in range(IDX_HEADS // 4):
            st = jnp.dot(ka, iqt_ref[:, hg * 4 * QB:(hg + 1) * 4 * QB], preferred_element_type=F32)
            for j in range(4):
                h = hg * 4 + j
                acc = acc + jnp.maximum(st[:, j * QB:(j + 1) * QB], 0.0) * wt_ref[h:h + 1, :]
        bits = pltpu.bitcast(acc, I32)
        key = bits ^ ((bits >> 31) & 0x7FFFFFFF)
        key_ref[c] = jnp.where(k0 + krow <= q_pos, key, INT_MIN)
        return carry

    lax.fori_loop(0, nck, score_chunk, 0)

    def count(pred):
        def body(c, cnts):
            cnts = list(cnts)
            for r in range(CK // SUBL):
                key = key_ref[c, r * SUBL:(r + 1) * SUBL, :]
                kidx = c * CK + r * SUBL + row8
                cnts[r % 4] = cnts[r % 4] + pred(key, kidx).astype(I32)
            return tuple(cnts)
        z = jnp.zeros((SUBL, QB), I32)
        cnts = lax.fori_loop(0, nck, body, (z, z, z, z))
        tot = (cnts[0] + cnts[1]) + (cnts[2] + cnts[3])
        return jnp.broadcast_to(jnp.sum(tot, axis=0, keepdims=True), (SUBL, QB))

    zero = jnp.zeros((SUBL, QB), I32)
    thr = jnp.where(count(lambda key, kidx: key >= zero) >= n_sel, zero, INT_MIN)

    def bit_step(b, thr):
        cand = thr | (1 << (30 - b))
        return jnp.where(count(lambda key, kidx: key >= cand) >= n_sel, cand, thr)

    thr = lax.fori_loop(0, 31, bit_step, thr)

    n_gt = count(lambda key, kidx: key > thr)
    n_ge = count(lambda key, kidx: key >= thr)
    need = n_sel - n_gt
    bound_ref[...] = jnp.full((SUBL, QB), 2 ** 30, I32)

    @pl.when(jnp.max(jnp.where(thr > INT_MIN, n_ge, 0)) > n_sel)
    def _():
        def idx_step(b, bound):
            cand = bound | (1 << (14 - b))
            n_tie = count(lambda key, kidx: (key == thr) & (kidx < cand))
            return jnp.where(n_tie <= need, cand, bound)
        bound_ref[...] = lax.fori_loop(0, 15, idx_step, zero)

    bound = bound_ref[...]

    def bias_chunk(c, carry):
        for r in range(CK // SUBL):
            key = key_ref[c, r * SUBL:(r + 1) * SUBL, :]
            kidx = c * CK + r * SUBL + row8
            sel = ((key > thr) | ((key == thr) & (kidx < bound))) & (kidx <= q_pos8)
            bias_ref[c, r * SUBL:(r + 1) * SUBL, :] = jnp.where(sel, 0.0, NEG_BIG)
        return carry

    lax.fori_loop(0, nck, bias_chunk, 0)

    m_ref[...] = jnp.full(m_ref.shape, NEG_BIG, F32)
    l_ref[...] = jnp.zeros(l_ref.shape, F32)
    acc_ref[...] = jnp.zeros(acc_ref.shape, F32)

    def attn_chunk(c, carry):
        k0 = pl.multiple_of(c * CK, CK)
        bias = jnp.concatenate([bias_ref[c]] * GROUP, axis=1)
        for kh in range(N_KV_HEADS):
            kc = k_ref[pl.ds(k0, CK), kh * HEAD_DIM:(kh + 1) * HEAD_DIM]
            st = jnp.dot(kc, qt_ref[:, kh * GW:(kh + 1) * GW], preferred_element_type=F32) + bias
            m_old = m_ref[kh]
            m_new = jnp.maximum(m_old, jnp.max(st, axis=0, keepdims=True))
            alpha = jnp.exp(m_old - m_new)
            p = jnp.exp(st - m_new)
            l_ref[kh] = alpha * l_ref[kh] + jnp.sum(p, axis=0, keepdims=True)
            vt = vt_ref[c, kh * HEAD_DIM:(kh + 1) * HEAD_DIM, :]
            acc_ref[kh] = alpha * acc_ref[kh] + jnp.dot(vt, p.astype(BF16),
                                                        preferred_element_type=F32)
            m_ref[kh] = m_new
        return carry

    lax.fori_loop(0, nck, attn_chunk, 0)
    for kh in range(N_KV_HEADS):
        ot = acc_ref[kh] / l_ref[kh]
        for g in range(GROUP):
            h = kh * GROUP + g
            o_ref[:, h * HEAD_DIM:(h + 1) * HEAD_DIM] = ot[:, g * QB:(g + 1) * QB].T.astype(o_ref.dtype)


def _sparse_attention(iq, iw, q, ika, k, vt, B, S):
    N = B * S
    nb = S // Q_BLOCK
    n_sel = min(IDX_TOPK_MAX, S // 4)
    kvw = N_KV_HEADS * HEAD_DIM
    nchunks = S // KEY_CHUNK
    gw = GROUP * Q_BLOCK
    qspec = lambda w: pl.BlockSpec((Q_BLOCK, w), lambda b, i: (b * nb + i, 0))
    kspec = lambda w: pl.BlockSpec((S, w), lambda b, i: (b, 0))
    return pl.pallas_call(
        functools.partial(_attn_kernel, n_sel),
        grid=(B, nb),
        in_specs=[qspec(IDX_WIDTH), qspec(LANES), qspec(ATTN_WIDTH), kspec(LANES), kspec(kvw),
                  pl.BlockSpec((nchunks, kvw, KEY_CHUNK), lambda b, i: (b, 0, 0))],
        out_specs=qspec(ATTN_WIDTH),
        out_shape=jax.ShapeDtypeStruct((N, ATTN_WIDTH), BF16),
        scratch_shapes=[pltpu.VMEM((nchunks, KEY_CHUNK, Q_BLOCK), I32),
                        pltpu.VMEM((nchunks, KEY_CHUNK, Q_BLOCK), F32),
                        pltpu.VMEM((LANES, IDX_HEADS * Q_BLOCK), BF16),
                        pltpu.VMEM((LANES, Q_BLOCK), F32),
                        pltpu.VMEM((HEAD_DIM, N_HEADS * Q_BLOCK), BF16),
                        pltpu.VMEM((8, Q_BLOCK), I32),
                        pltpu.VMEM((N_KV_HEADS, 1, gw), F32),
                        pltpu.VMEM((N_KV_HEADS, 1, gw), F32),
                        pltpu.VMEM((N_KV_HEADS, HEAD_DIM, gw), F32)],
        compiler_params=_params(("arbitrary", "arbitrary")),
        name="dsa_attention",
    )(iq, iw, q, ika, k, vt)


def _split_bf16(a):
    hi = a.astype(BF16)
    lo = (a - hi.astype(F32)).astype(BF16)
    return hi, lo


def _mix_kernel(yc_ref, ao_ref, gate_ref, x_ref, g1_ref, wao_ref, wo_ref, n2_ref, sc2_ref, sh2_ref,
                wrh_ref, wrl_ref, br_ref, x1_ref, h2_ref, idx_ref, prob_ref):
    ya = jnp.dot(ao_ref[...], wao_ref[...], preferred_element_type=F32)
    mixed = yc_ref[...].astype(F32) + gate_ref[...].astype(F32) * ya
    z = jnp.dot(mixed.astype(BF16), wo_ref[...], preferred_element_type=F32)
    x1 = x_ref[...] + g1_ref[0] * z
    x1_ref[...] = x1
    ms = jnp.mean(x1 * x1, axis=-1, keepdims=True)
    h2 = x1 * lax.rsqrt(ms + EPS) * n2_ref[...] * (1.0 + sc2_ref[0]) + sh2_ref[0]
    h2_ref[...] = h2.astype(h2_ref.dtype)
    hh, hl = _split_bf16(h2)
    logits = (jnp.dot(hh, wrh_ref[...], preferred_element_type=F32)
              + jnp.dot(hh, wrl_ref[...], preferred_element_type=F32)
              + jnp.dot(hl, wrh_ref[...], preferred_element_type=F32)) + br_ref[...]
    lane = lax.broadcasted_iota(I32, logits.shape, 1)
    work = jnp.where(lane < N_EXPERTS, logits, -jnp.inf)
    vals = []
    idx_out = jnp.zeros(logits.shape, I32)
    for k in range(TOP_K):
        mv = jnp.max(work, axis=-1, keepdims=True)
        mi = jnp.min(jnp.where(work == mv, lane, LANES), axis=-1, keepdims=True)
        idx_out = jnp.where(lane == k, mi, idx_out)
        work = jnp.where(lane == mi, -jnp.inf, work)
        vals.append(mv)
    es = [jnp.exp(vk - vals[0]) for vk in vals]
    den = es[0]
    for e in es[1:]:
        den = den + e
    prob_out = jnp.zeros(logits.shape, F32)
    for k in range(TOP_K):
        prob_out = jnp.where(lane == k, es[k] / den, prob_out)
    idx_ref[...] = idx_out
    prob_ref[...] = prob_out


def _mix(yc, ao, gates, x, g1, w_attn_out, w_out, norm2_g, sc2, sh2, w_router, b_router, B, S):
    N, D = x.shape
    tm = _tile(S, 512)
    nS = S // tm
    wr = jnp.zeros((D, LANES), F32).at[:, :N_EXPERTS].set(w_router)
    wrh, wrl = _split_bf16(wr)
    br = jnp.zeros((1, LANES), F32).at[0, :N_EXPERTS].set(b_router)
    row = lambda w, c=0: pl.BlockSpec((tm, w), lambda i: (i, c))
    full = lambda a: pl.BlockSpec(a.shape, lambda i: (0,) * a.ndim)
    per_b = pl.BlockSpec((1, 1, D), lambda i: (i // nS, 0, 0))
    wao = w_attn_out.astype(BF16)
    wo = w_out.astype(BF16)
    n2 = norm2_g.reshape(1, D)
    return pl.pallas_call(
        _mix_kernel,
        grid=(N // tm,),
        in_specs=[row(D), row(ATTN_WIDTH), row(D, 1), row(D), per_b, full(wao), full(wo),
                  full(n2), per_b, per_b, full(wrh), full(wrl), full(br)],
        out_specs=[row(D), row(D), row(LANES), row(LANES)],
        out_shape=[jax.ShapeDtypeStruct((N, D), F32), jax.ShapeDtypeStruct((N, D), BF16),
                   jax.ShapeDtypeStruct((N, LANES), I32), jax.ShapeDtypeStruct((N, LANES), F32)],
        compiler_params=_params(("arbitrary",)),
        name="mix_router",
    )(yc, ao, gates, x, g1.reshape(B, 1, D), wao, wo, n2, sc2.reshape(B, 1, D),
      sh2.reshape(B, 1, D), wrh, wrl, br)


def _moe_kernel(blk_ref, exp_ref, nsub_ref, x_ref, rw_ref, wg_ref, bg_ref, wu_ref, bu_ref,
                wd_ref, bd_ref, o_ref, wg_s, wu_s, wd_s, acc_ref):
    w = pl.program_id(0)
    f = pl.program_id(1)
    nf = pl.num_programs(1)
    nsub = nsub_ref[w]

    @pl.when(nsub > 0)
    def _():
        wg_s[...] = wg_ref[0].astype(BF16)
        wu_s[...] = wu_ref[0].astype(BF16)
        wd_s[...] = wd_ref[0].astype(BF16)

        @pl.when(f == 0)
        def _():
            def zero(s, carry):
                r0 = pl.multiple_of(s * MOE_SUB, MOE_SUB)
                acc_ref[pl.ds(r0, MOE_SUB), :] = jnp.zeros((MOE_SUB, acc_ref.shape[1]), F32)
                return carry
            lax.fori_loop(0, nsub, zero, 0)

        def body(s, carry):
            r0 = pl.multiple_of(s * MOE_SUB, MOE_SUB)
            xs = x_ref[pl.ds(r0, MOE_SUB), :]
            g = jnp.dot(xs, wg_s[...], preferred_element_type=F32) + bg_ref[0]
            u = jnp.dot(xs, wu_s[...], preferred_element_type=F32) + bu_ref[0]
            g = jnp.minimum(g, SWIGLU_LIMIT)
            u = jnp.clip(u, -SWIGLU_LIMIT, SWIGLU_LIMIT)
            act = g * jax.nn.sigmoid(SWIGLU_ALPHA * g) * (u + 1.0)
            acc_ref[pl.ds(r0, MOE_SUB), :] += jnp.dot(act.astype(BF16), wd_s[...],
                                                      preferred_element_type=F32)
            return carry
        lax.fori_loop(0, nsub, body, 0)

        @pl.when(f == nf - 1)
        def _():
            def fin(s, carry):
                r0 = pl.multiple_of(s * MOE_SUB, MOE_SUB)
                y = (acc_ref[pl.ds(r0, MOE_SUB), :] + bd_ref[0]) * rw_ref[pl.ds(r0, MOE_SUB), :]
                o_ref[pl.ds(r0, MOE_SUB), :] = y.astype(o_ref.dtype)
                return carry
            lax.fori_loop(0, nsub, fin, 0)


def _moe_experts(xs, row_w, item_blk, item_exp, item_nsub, w_gate, b_gate, w_up, b_up,
                 w_down, b_down):
    P, D = xs.shape
    E, _, F = w_gate.shape
    n_items = item_blk.shape[0]
    tf = _tile(F, MOE_FCHUNK)
    nf = F // tf

    def fidx(f, ns):
        return jnp.where(ns > 0, f, nf - 1)

    grid_spec = pltpu.PrefetchScalarGridSpec(
        num_scalar_prefetch=3,
        grid=(n_items, nf),
        in_specs=[
            pl.BlockSpec((MOE_TILE, D), lambda w, f, blk, ex, ns: (blk[w], 0)),
            pl.BlockSpec((MOE_TILE, 1), lambda w, f, blk, ex, ns: (blk[w], 0)),
            pl.BlockSpec((1, D, tf), lambda w, f, blk, ex, ns: (ex[w], 0, fidx(f, ns[w]))),
            pl.BlockSpec((1, 1, tf), lambda w, f, blk, ex, ns: (ex[w], 0, fidx(f, ns[w]))),
            pl.BlockSpec((1, D, tf), lambda w, f, blk, ex, ns: (ex[w], 0, fidx(f, ns[w]))),
            pl.BlockSpec((1, 1, tf), lambda w, f, blk, ex, ns: (ex[w], 0, fidx(f, ns[w]))),
            pl.BlockSpec((1, tf, D), lambda w, f, blk, ex, ns: (ex[w], fidx(f, ns[w]), 0)),
            pl.BlockSpec((1, 1, D), lambda w, f, blk, ex, ns: (ex[w], 0, 0)),
        ],
        out_specs=pl.BlockSpec((MOE_TILE, D), lambda w, f, blk, ex, ns: (blk[w], 0)),
        scratch_shapes=[pltpu.VMEM((D, tf), BF16), pltpu.VMEM((D, tf), BF16),
                        pltpu.VMEM((tf, D), BF16), pltpu.VMEM((MOE_TILE, D), F32)],
    )
    return pl.pallas_call(
        _moe_kernel,
        grid_spec=grid_spec,
        out_shape=jax.ShapeDtypeStruct((P, D), BF16),
        compiler_params=_params(("arbitrary", "arbitrary")),
        name="moe_experts",
    )(item_blk, item_exp, item_nsub, xs, row_w, w_gate, b_gate.reshape(E, 1, F), w_up,
      b_up.reshape(E, 1, F), w_down, b_down.reshape(E, 1, D))


def _route(top_idx, probs):
    N = top_idx.shape[0]
    A = N * TOP_K
    n_items = A // MOE_TILE + N_EXPERTS
    P = n_items * MOE_TILE
    flat_e = top_idx.reshape(A)
    onehot = (flat_e[:, None] == jnp.arange(N_EXPERTS, dtype=I32)[None, :]).astype(I32)
    csum = jnp.cumsum(onehot, axis=0)
    counts = csum[-1]
    rank = jnp.sum(onehot * csum, axis=1) - 1
    padded = (counts + MOE_TILE - 1) // MOE_TILE * MOE_TILE
    pad_end = jnp.cumsum(padded)
    pad_start = pad_end - padded
    grp_start = jnp.cumsum(counts) - counts
    dest = pad_start[flat_e] + rank
    order = jnp.argsort(flat_e)
    rows = jnp.arange(P, dtype=I32)
    row_e = jnp.minimum(jnp.searchsorted(pad_end, rows, side='right'), N_EXPERTS - 1).astype(I32)
    off = rows - pad_start[row_e]
    valid = off < counts[row_e]
    src = order[jnp.clip(grp_start[row_e] + off, 0, A - 1)]
    row_tok = jnp.where(valid, src // TOP_K, 0).astype(I32)
    row_w = jnp.where(valid, probs.reshape(A)[src], 0.0).astype(F32)
    items = jnp.arange(n_items, dtype=I32)
    n_used = (pad_end[-1] // MOE_TILE).astype(I32)
    last = jnp.maximum(n_used - 1, 0)
    item_blk = jnp.minimum(items, last)
    item_exp = row_e[item_blk * MOE_TILE]
    filled = jnp.clip(counts[item_exp] - (item_blk * MOE_TILE - pad_start[item_exp]), 0, MOE_TILE)
    item_nsub = jnp.where(items < n_used, (filled + MOE_SUB - 1) // MOE_SUB, 0).astype(I32)
    return row_tok, row_w, dest.astype(I32), item_blk, item_exp, item_nsub


def _final_kernel(x1_ref, y_ref, g2_ref, o_ref):
    y = y_ref[0].astype(F32)
    for k in range(1, TOP_K):
        y = y + y_ref[k].astype(F32)
    o_ref[...] = x1_ref[...] + g2_ref[0] * y


def _final(x1, yk, g2, B, S):
    N, D = x1.shape
    tm = _tile(S, 512)
    nS = S // tm
    return pl.pallas_call(
        _final_kernel,
        grid=(N // tm,),
        in_specs=[pl.BlockSpec((tm, D), lambda i: (i, 0)),
                  pl.BlockSpec((TOP_K, tm, D), lambda i: (0, i, 0)),
                  pl.BlockSpec((1, 1, D), lambda i: (i // nS, 0, 0))],
        out_specs=pl.BlockSpec((tm, D), lambda i: (i, 0)),
        out_shape=jax.ShapeDtypeStruct((N, D), F32),
        compiler_params=_params(("arbitrary",)),
        name="moe_combine",
    )(x1, yk, g2.reshape(B, 1, D))


def _layer(x, c, w_ada, b_ada, norm1_g, w_in, conv_w, conv_b, conv_ln_g, conv_ln_b, w_conv_out,
           q_norm_g, k_norm_g, w_attn_out, w_out, norm2_g, w_router, b_router, w_gate, b_gate,
           w_up, b_up, w_down, b_down):
    B, S, D = x.shape
    N = B * S
    C = conv_w.shape[1]
    kvw = N_KV_HEADS * HEAD_DIM

    mod = _ada(c, w_ada, b_ada)
    sh1, sc1, g1, sh2, sc2, g2 = [mod[:, j * D:(j + 1) * D] for j in range(6)]

    h = _normmod(x, norm1_g, sc1, sh1).reshape(N, D)

    o = 0
    w_conv_in = w_in[:, o:o + 2 * C]; o += 2 * C
    w_q = w_in[:, o:o + ATTN_WIDTH]; o += ATTN_WIDTH
    w_kv_small = w_in[:, o:o + 2 * kvw + IDX_WIDTH + IDX_DIM + IDX_HEADS]
    w_k_v = w_in[:, o:o + 2 * kvw]; o += 2 * kvw
    w_iq = w_in[:, o:o + IDX_WIDTH]; o += IDX_WIDTH
    w_ik_iw = w_in[:, o:o + IDX_DIM + IDX_HEADS]; o += IDX_DIM + IDX_HEADS
    w_gates = w_in[:, o:o + 2 * D]
    del w_kv_small

    tm = _tile(S, 1024)
    nS = S // tm
    cos128, sin128 = _rope_tables(S, HEAD_DIM)
    cos64, sin64 = _rope_tables(S, IDX_DIM)
    tab = pl.BlockSpec((tm, LANES), lambda i, j: (i % nS, 0))
    vec = pl.BlockSpec((1, LANES), lambda i, j: (0, 0))

    tn = _tile(2 * C, 1024)
    hc = tn // 2
    wv = w_conv_in[:, :C].reshape(D, C // hc, hc)
    wg = w_conv_in[:, C:].reshape(D, C // hc, hc)
    w_glu = jnp.concatenate([wv, wg], axis=2).reshape(D, 2 * C).astype(BF16)
    u = _proj(_glu_kernel, h, w_glu, tn, [], [],
              jax.ShapeDtypeStruct((N, C), BF16),
              pl.BlockSpec((tm, hc), lambda i, j: (i, j)), tm, "proj_conv_glu")

    tq = _tile(ATTN_WIDTH, 512)
    q = _proj(_q_kernel, h, w_q.astype(BF16), tq, [cos128, sin128, q_norm_g.reshape(1, HEAD_DIM)],
              [tab, tab, vec], jax.ShapeDtypeStruct((N, ATTN_WIDTH), BF16),
              pl.BlockSpec((tm, tq), lambda i, j: (i, j)), tm, "proj_q")

    pad = LANES - IDX_DIM - IDX_HEADS
    w_kvs = jnp.concatenate([w_k_v, w_ik_iw, jnp.zeros((D, pad), F32)], axis=1).astype(BF16)
    nk = w_kvs.shape[1]
    blk = lambda w: pl.BlockSpec((tm, w), lambda i, j: (i, 0))
    cpt = tm // KEY_CHUNK
    k, vt, ika, iw = _proj(
        _kv_kernel, h, w_kvs, nk,
        [cos128, sin128, cos64, sin64, k_norm_g.reshape(1, HEAD_DIM)], [tab, tab, tab, tab, vec],
        [jax.ShapeDtypeStruct((N, kvw), BF16), jax.ShapeDtypeStruct((N // KEY_CHUNK, kvw, KEY_CHUNK), BF16),
         jax.ShapeDtypeStruct((N, LANES), BF16), jax.ShapeDtypeStruct((N, LANES), F32)],
        [blk(kvw), pl.BlockSpec((cpt, kvw, KEY_CHUNK), lambda i, j: (i, 0, 0)), blk(LANES), blk(LANES)],
        tm, "proj_kv_indexer_key")

    ti = _tile(IDX_WIDTH, 512)
    iq = _proj(_iq_kernel, h, w_iq.astype(BF16), ti, [cos64, sin64], [tab, tab],
               jax.ShapeDtypeStruct((N, IDX_WIDTH), BF16),
               pl.BlockSpec((tm, ti), lambda i, j: (i, j)), tm, "proj_iq")

    tg = _tile(2 * D, 1024)
    gates = _proj(_sigmoid_kernel, h, w_gates.astype(BF16), tg, [], [],
                  jax.ShapeDtypeStruct((N, 2 * D), BF16),
                  pl.BlockSpec((tm, tg), lambda i, j: (i, j)), tm, "proj_gates")

    yc = _conv_branch(u.reshape(B, S, C), conv_w, conv_b, conv_ln_g, conv_ln_b, w_conv_out,
                      gates.reshape(B, S, 2 * D)).reshape(N, D)
    ao = _sparse_attention(iq, iw, q, ika, k, vt, B, S)

    x1, h2, top_idx, probs = _mix(yc, ao, gates, x.reshape(N, D), g1, w_attn_out, w_out, norm2_g,
                                  sc2, sh2, w_router, b_router, B, S)

    row_tok, row_w, dest, item_blk, item_exp, item_nsub = _route(top_idx[:, :TOP_K],
                                                                 probs[:, :TOP_K])
    xs = jnp.take(h2, row_tok, axis=0)
    ys = _moe_experts(xs, row_w[:, None], item_blk, item_exp, item_nsub, w_gate, b_gate, w_up,
                      b_up, w_down, b_down)
    yk = jnp.take(ys, dest.reshape(N, TOP_K).T, axis=0)
    out = _final(x1, yk, g2, B, S)
    return out.reshape(B, S, D)


def kernel(x, c, w_ada, b_ada, norm1_g, w_in, conv_w, conv_b, conv_ln_g, conv_ln_b, w_conv_out,
           q_norm_g, k_norm_g, w_attn_out, w_out, norm2_g, w_router, b_router, w_gate, b_gate,
           w_up, b_up, w_down, b_down):
    for l in range(w_ada.shape[0]):
        x = _layer(x, c, w_ada[l], b_ada[l], norm1_g[l], w_in[l], conv_w[l], conv_b[l],
                   conv_ln_g[l], conv_ln_b[l], w_conv_out[l], q_norm_g[l], k_norm_g[l],
                   w_attn_out[l], w_out[l], norm2_g[l], w_router[l], b_router[l], w_gate[l],
                   b_gate[l], w_up[l], b_up[l], w_down[l], b_down[l])
    return x
```

```python
import functools

import jax
import jax.numpy as jnp
from jax import lax
from jax.experimental import pallas as pl
from jax.experimental.pallas import tpu as pltpu

F32 = jnp.float32
BF16 = jnp.bfloat16
I32 = jnp.int32

EPS = 1e-6
CONV_TAPS = 31
N_HEADS = 8
N_KV_HEADS = 2
HEAD_DIM = 128
GROUP = N_HEADS // N_KV_HEADS
ATTN_WIDTH = N_HEADS * HEAD_DIM
ROPE_THETA = 10000.0
Q_BLOCK = 128
IDX_HEADS = 16
IDX_DIM = 64
IDX_WIDTH = IDX_HEADS * IDX_DIM
IDX_TOPK_MAX = 256
N_EXPERTS = 32
TOP_K = 4
SWIGLU_ALPHA = 1.702
SWIGLU_LIMIT = 7.0

LANES = 128
CONV_HALO = 32
KEY_CHUNK = 256
MOE_TILE = 1024
MOE_SUB = 256
MOE_FCHUNK = 512
NEG_BIG = -1e30
INT_MIN = -2 ** 31
VMEM_LIMIT = 60 * 1024 * 1024


def _tile(dim, pref):
    if dim <= pref:
        return dim
    t = pref - pref % LANES
    while t >= LANES:
        if dim % t == 0:
            return t
        t -= LANES
    return dim


def _params(sem):
    return pltpu.CompilerParams(dimension_semantics=sem, vmem_limit_bytes=VMEM_LIMIT)


def _ada_kernel(c_ref, w_ref, b_ref, o_ref):
    c = c_ref[...]
    sc = c * jax.nn.sigmoid(c)
    o_ref[...] = jnp.dot(sc.astype(BF16), w_ref[...].astype(BF16),
                         preferred_element_type=F32) + b_ref[...]


def _ada(c, w_ada, b_ada):
    B, D = c.shape
    n = w_ada.shape[1]
    rows = 8
    cp = jnp.zeros((rows, D), F32).at[:B].set(c)
    tn = _tile(n, 1024)
    out = pl.pallas_call(
        _ada_kernel,
        grid=(n // tn,),
        in_specs=[pl.BlockSpec((rows, D), lambda j: (0, 0)),
                  pl.BlockSpec((D, tn), lambda j: (0, j)),
                  pl.BlockSpec((1, tn), lambda j: (0, j))],
        out_specs=pl.BlockSpec((rows, tn), lambda j: (0, j)),
        out_shape=jax.ShapeDtypeStruct((rows, n), F32),
        compiler_params=_params(("arbitrary",)),
        name="ada_mod",
    )(cp, w_ada, b_ada.reshape(1, n))
    return out[:B]


def _normmod_kernel(x_ref, g_ref, sc_ref, sh_ref, o_ref):
    x = x_ref[0]
    ms = jnp.mean(x * x, axis=-1, keepdims=True)
    y = x * lax.rsqrt(ms + EPS) * g_ref[...]
    o_ref[0] = (y * (1.0 + sc_ref[0]) + sh_ref[0]).astype(o_ref.dtype)


def _normmod(x, g, sc, sh):
    B, S, D = x.shape
    ts = _tile(S, 512)
    return pl.pallas_call(
        _normmod_kernel,
        grid=(B, S // ts),
        in_specs=[pl.BlockSpec((1, ts, D), lambda b, i: (b, i, 0)),
                  pl.BlockSpec((1, D), lambda b, i: (0, 0)),
                  pl.BlockSpec((1, 1, D), lambda b, i: (b, 0, 0)),
                  pl.BlockSpec((1, 1, D), lambda b, i: (b, 0, 0))],
        out_specs=pl.BlockSpec((1, ts, D), lambda b, i: (b, i, 0)),
        out_shape=jax.ShapeDtypeStruct((B, S, D), BF16),
        compiler_params=_params(("arbitrary", "arbitrary")),
        name="norm1_mod",
    )(x, g.reshape(1, D), sc.reshape(B, 1, D), sh.reshape(B, 1, D))


def _rope_tables(S, d):
    inv = ROPE_THETA ** (-jnp.arange(0, d, 2, dtype=F32) / d)
    ang = jnp.arange(S, dtype=F32)[:, None] * inv[None, :]
    cos = jnp.cos(ang)
    sin = jnp.sin(ang)
    reps = LANES // d
    cos_t = jnp.tile(jnp.concatenate([cos, cos], axis=1), (1, reps))
    sin_t = jnp.tile(jnp.concatenate([-sin, sin], axis=1), (1, reps))
    return cos_t, sin_t


def _rope128(y, cos, sin):
    return y * cos + pltpu.roll(y, 64, axis=1) * sin


def _rope64(y, cos, sin):
    lane = lax.broadcasted_iota(I32, y.shape, 1)
    rot = jnp.where((lane % 64) < 32, pltpu.roll(y, 96, axis=1), pltpu.roll(y, 32, axis=1))
    return y * cos + rot * sin


def _glu_kernel(x_ref, w_ref, o_ref):
    acc = jnp.dot(x_ref[...], w_ref[...], preferred_element_type=F32)
    half = acc.shape[1] // 2
    o_ref[...] = (acc[:, :half] * jax.nn.sigmoid(acc[:, half:])).astype(o_ref.dtype)


def _q_kernel(x_ref, w_ref, cos_ref, sin_ref, g_ref, o_ref):
    acc = jnp.dot(x_ref[...], w_ref[...], preferred_element_type=F32)
    cos = cos_ref[...]
    sin = sin_ref[...]
    g = g_ref[...]
    for h in range(acc.shape[1] // HEAD_DIM):
        xh = acc[:, h * HEAD_DIM:(h + 1) * HEAD_DIM]
        ms = jnp.mean(xh * xh, axis=-1, keepdims=True)
        y = xh * lax.rsqrt(ms + EPS) * g
        o_ref[:, h * HEAD_DIM:(h + 1) * HEAD_DIM] = (
            _rope128(y, cos, sin) * (HEAD_DIM ** -0.5)).astype(o_ref.dtype)


def _kv_kernel(x_ref, w_ref, cos_ref, sin_ref, cos64_ref, sin64_ref, g_ref,
               k_ref, vt_ref, ika_ref, iw_ref):
    acc = jnp.dot(x_ref[...], w_ref[...], preferred_element_type=F32)
    kvw = N_KV_HEADS * HEAD_DIM
    cos = cos_ref[...]
    sin = sin_ref[...]
    g = g_ref[...]
    for h in range(N_KV_HEADS):
        xh = acc[:, h * HEAD_DIM:(h + 1) * HEAD_DIM]
        ms = jnp.mean(xh * xh, axis=-1, keepdims=True)
        y = xh * lax.rsqrt(ms + EPS) * g
        k_ref[:, h * HEAD_DIM:(h + 1) * HEAD_DIM] = _rope128(y, cos, sin).astype(k_ref.dtype)
    vt = acc[:, kvw:2 * kvw].T.astype(vt_ref.dtype)
    for c in range(vt_ref.shape[0]):
        vt_ref[c] = vt[:, c * KEY_CHUNK:(c + 1) * KEY_CHUNK]
    tail = acc[:, 2 * kvw:2 * kvw + LANES]
    lane = lax.broadcasted_iota(I32, tail.shape, 1)
    roped = _rope64(tail, cos64_ref[...], sin64_ref[...])
    ika_ref[...] = jnp.where(lane < IDX_DIM, roped, 0.0).astype(ika_ref.dtype)
    iw = pltpu.roll(tail, LANES - IDX_DIM, axis=1)
    iw_ref[...] = jnp.where(lane < IDX_HEADS, iw * (IDX_HEADS ** -0.5), 0.0)


def _iq_kernel(x_ref, w_ref, cos_ref, sin_ref, o_ref):
    acc = jnp.dot(x_ref[...], w_ref[...], preferred_element_type=F32)
    cos = cos_ref[...]
    sin = sin_ref[...]
    for p in range(acc.shape[1] // LANES):
        y = acc[:, p * LANES:(p + 1) * LANES]
        o_ref[:, p * LANES:(p + 1) * LANES] = (
            _rope64(y, cos, sin) * (IDX_DIM ** -0.5)).astype(o_ref.dtype)


def _sigmoid_kernel(x_ref, w_ref, o_ref):
    acc = jnp.dot(x_ref[...], w_ref[...], preferred_element_type=F32)
    o_ref[...] = jax.nn.sigmoid(acc).astype(o_ref.dtype)


def _proj(kernel, h, w, tn, aux, aux_specs, out_shapes, out_specs, tm, name):
    N, D = h.shape
    n = w.shape[1]
    return pl.pallas_call(
        kernel,
        grid=(N // tm, n // tn),
        in_specs=[pl.BlockSpec((tm, D), lambda i, j: (i, 0)),
                  pl.BlockSpec((D, tn), lambda i, j: (0, j))] + aux_specs,
        out_specs=out_specs,
        out_shape=out_shapes,
        compiler_params=_params(("arbitrary", "arbitrary")),
        name=name,
    )(h, w, *aux)


def _conv_kernel(u_ref, halo_ref, cw_ref, cb_ref, lg_ref, lb_ref, wo_ref, gate_ref, o_ref,
                 buf_ref, sh_ref, conv_ref):
    i = pl.program_id(1)
    ts = u_ref.shape[1]
    SUBL = 8
    halo = halo_ref[0].astype(F32)
    buf_ref[0:CONV_HALO, :] = jnp.where(i == 0, 0.0, halo)
    buf_ref[CONV_HALO:, :] = u_ref[0].astype(F32)
    for b in range(1, SUBL):
        sh_ref[b - 1] = buf_ref[b:b + sh_ref.shape[1], :]
    rc = 32
    first = CONV_HALO - (CONV_TAPS - 1)

    def chunk(c, carry):
        r0 = pl.multiple_of(c * rc, rc)
        acc = jnp.zeros((rc, buf_ref.shape[1]), F32)
        for t in range(CONV_TAPS):
            a, b = divmod(first + t, SUBL)
            if b == 0:
                rows = buf_ref[pl.ds(r0 + a * SUBL, rc), :]
            else:
                rows = sh_ref[b - 1, pl.ds(r0 + a * SUBL, rc), :]
            acc = acc + cw_ref[t:t + 1, :] * rows
        conv_ref[pl.ds(r0, rc), :] = acc
        return carry

    lax.fori_loop(0, ts // rc, chunk, 0)
    u = conv_ref[...] + cb_ref[...]
    mu = jnp.mean(u, axis=-1, keepdims=True)
    uc = u - mu
    var = jnp.mean(uc * uc, axis=-1, keepdims=True)
    y = uc * lax.rsqrt(var + EPS) * lg_ref[...] + lb_ref[...]
    y = y * jax.nn.sigmoid(y)
    z = jnp.dot(y.astype(BF16), wo_ref[...], preferred_element_type=F32)
    o_ref[0] = (z * gate_ref[0].astype(F32)).astype(o_ref.dtype)


def _conv_branch(u, conv_w, conv_b, ln_g, ln_b, w_conv_out, gates):
    B, S, C = u.shape
    D = w_conv_out.shape[1]
    ts = _tile(S, 256)
    hb = ts // CONV_HALO
    cw = jnp.zeros((CONV_HALO, C), F32).at[:CONV_TAPS].set(conv_w)
    return pl.pallas_call(
        _conv_kernel,
        grid=(B, S // ts),
        in_specs=[pl.BlockSpec((1, ts, C), lambda b, i: (b, i, 0)),
                  pl.BlockSpec((1, CONV_HALO, C), lambda b, i: (b, jnp.maximum(i * hb - 1, 0), 0)),
                  pl.BlockSpec((CONV_HALO, C), lambda b, i: (0, 0)),
                  pl.BlockSpec((1, C), lambda b, i: (0, 0)),
                  pl.BlockSpec((1, C), lambda b, i: (0, 0)),
                  pl.BlockSpec((1, C), lambda b, i: (0, 0)),
                  pl.BlockSpec((C, D), lambda b, i: (0, 0)),
                  pl.BlockSpec((1, ts, D), lambda b, i: (b, i, 0))],
        out_specs=pl.BlockSpec((1, ts, D), lambda b, i: (b, i, 0)),
        out_shape=jax.ShapeDtypeStruct((B, S, D), BF16),
        scratch_shapes=[pltpu.VMEM((ts + CONV_HALO, C), F32),
                        pltpu.VMEM((7, ts + CONV_HALO - 8, C), F32),
                        pltpu.VMEM((ts, C), F32)],
        compiler_params=_params(("arbitrary", "arbitrary")),
        name="conv_branch",
    )(u, u, cw, conv_b.reshape(1, C), ln_g.reshape(1, C), ln_b.reshape(1, C),
      w_conv_out.astype(BF16), gates)


def _attn_kernel(n_sel, iq_ref, iw_ref, q_ref, ika_ref, k_ref, vt_ref, o_ref,
                 key_ref, bias_ref, iqt_ref, wt_ref, qt_ref, bound_ref, s_ref, acc_ref):
    i = pl.program_id(1)
    QB = Q_BLOCK
    CK = KEY_CHUNK
    GW = GROUP * QB
    SUBL = 8
    nck = (i * QB + QB + CK - 1) // CK
    q_pos = i * QB + lax.broadcasted_iota(I32, (CK, QB), 1)
    krow = lax.broadcasted_iota(I32, (CK, QB), 0)
    q_pos8 = q_pos[0:SUBL]
    row8 = krow[0:SUBL]

    iq = iq_ref[...].astype(F32)
    iqt_ref[IDX_DIM:, :] = jnp.zeros((LANES - IDX_DIM, IDX_HEADS * QB), BF16)
    for p in range(IDX_HEADS // 2):
        t = iq[:, p * LANES:(p + 1) * LANES].T
        iqt_ref[0:IDX_DIM, (2 * p) * QB:(2 * p + 1) * QB] = t[0:IDX_DIM].astype(BF16)
        iqt_ref[0:IDX_DIM, (2 * p + 1) * QB:(2 * p + 2) * QB] = t[IDX_DIM:].astype(BF16)
    wt_ref[...] = iw_ref[...].T
    qf = q_ref[...].astype(F32)
    for h in range(N_HEADS):
        qt_ref[:, h * QB:(h + 1) * QB] = qf[:, h * HEAD_DIM:(h + 1) * HEAD_DIM].T.astype(BF16)

    def score_chunk(c, carry):
        k0 = pl.multiple_of(c * CK, CK)
        ka = ika_ref[pl.ds(k0, CK), :]
        acc = jnp.zeros((CK, QB), F32)
        for hg in range(IDX_HEADS // 4):
            st = jnp.dot(ka, iqt_ref[:, hg * 4 * QB:(hg + 1) * 4 * QB], preferred_element_type=F32)
            for j in range(4):
                h = hg * 4 + j
                acc = acc + jnp.maximum(st[:, j * QB:(j + 1) * QB], 0.0) * wt_ref[h:h + 1, :]
        key_ref[c] = jnp.where(k0 + krow <= q_pos, acc, -jnp.inf)
        return carry

    lax.fori_loop(0, nck, score_chunk, 0)

    def count(pred):
        def body(c, cnts):
            cnts = list(cnts)
            for r in range(CK // SUBL):
                sc = key_ref[c, r * SUBL:(r + 1) * SUBL, :]
                kidx = c * CK + r * SUBL + row8
                cnts[r % 4] = cnts[r % 4] + pred(sc, kidx).astype(I32)
            return tuple(cnts)
        z = jnp.zeros((SUBL, QB), I32)
        cnts = lax.fori_loop(0, nck, body, (z, z, z, z))
        tot = (cnts[0] + cnts[1]) + (cnts[2] + cnts[3])
        return jnp.broadcast_to(jnp.sum(tot, axis=0, keepdims=True), (SUBL, QB))

    def code_value(code):
        return pltpu.bitcast(code ^ ((code >> 31) & 0x7FFFFFFF), F32)

    zero = jnp.zeros((SUBL, QB), I32)
    zero_f = jnp.zeros((SUBL, QB), F32)
    code = jnp.where(count(lambda sc, kidx: sc >= zero_f) >= n_sel, zero, INT_MIN)

    def bit_step(b, code):
        cand = code | (1 << (30 - b))
        cand_f = code_value(cand)
        return jnp.where(count(lambda sc, kidx: sc >= cand_f) >= n_sel, cand, code)

    code = lax.fori_loop(0, 31, bit_step, code)
    thr = jnp.where(code == INT_MIN, -jnp.inf, code_value(code))

    n_gt = count(lambda sc, kidx: sc > thr)
    n_ge = count(lambda sc, kidx: sc >= thr)
    need = n_sel - n_gt
    bound_ref[...] = jnp.full((SUBL, QB), 2 ** 30, I32)

    @pl.when(jnp.max(jnp.where(code > INT_MIN, n_ge, 0)) > n_sel)
    def _():
        def idx_step(b, bound):
            cand = bound | (1 << (14 - b))
            n_tie = count(lambda sc, kidx: (sc == thr) & (kidx < cand))
            return jnp.where(n_tie <= need, cand, bound)
        bound_ref[...] = lax.fori_loop(0, 15, idx_step, zero)

    bound = bound_ref[...]

    def bias_chunk(c, carry):
        for r in range(CK // SUBL):
            sc = key_ref[c, r * SUBL:(r + 1) * SUBL, :]
            kidx = c * CK + r * SUBL + row8
            sel = ((sc > thr) | ((sc == thr) & (kidx < bound))) & (kidx <= q_pos8)
            bias_ref[c, r * SUBL:(r + 1) * SUBL, :] = jnp.where(sel, 0.0, NEG_BIG)
        return carry

    lax.fori_loop(0, nck, bias_chunk, 0)

    def logit_chunk(c, mx):
        k0 = pl.multiple_of(c * CK, CK)
        bias = jnp.concatenate([bias_ref[c]] * GROUP, axis=1)
        out = []
        for kh in range(N_KV_HEADS):
            kc = k_ref[pl.ds(k0, CK), kh * HEAD_DIM:(kh + 1) * HEAD_DIM]
            st = jnp.dot(kc, qt_ref[:, kh * GW:(kh + 1) * GW], preferred_element_type=F32) + bias
            s_ref[c, :, kh * GW:(kh + 1) * GW] = st
            m8 = mx[kh]
            for r in range(CK // SUBL):
                m8 = jnp.maximum(m8, st[r * SUBL:(r + 1) * SUBL, :])
            out.append(m8)
        return tuple(out)

    mx = lax.fori_loop(0, nck, logit_chunk,
                       tuple(jnp.full((SUBL, GW), NEG_BIG, F32) for _ in range(N_KV_HEADS)))
    m_fin = [jnp.max(m8, axis=0, keepdims=True) for m8 in mx]

    acc_ref[...] = jnp.zeros(acc_ref.shape, F32)

    def value_chunk(c, ls):
        out = []
        for kh in range(N_KV_HEADS):
            p = jnp.exp(s_ref[c, :, kh * GW:(kh + 1) * GW] - m_fin[kh])
            l8 = ls[kh]
            for r in range(CK // SUBL):
                l8 = l8 + p[r * SUBL:(r + 1) * SUBL, :]
            out.append(l8)
            vt = vt_ref[c, kh * HEAD_DIM:(kh + 1) * HEAD_DIM, :]
            acc_ref[kh] += jnp.dot(vt, p.astype(BF16), preferred_element_type=F32)
        return tuple(out)

    ls = lax.fori_loop(0, nck, value_chunk,
                       tuple(jnp.zeros((SUBL, GW), F32) for _ in range(N_KV_HEADS)))
    for kh in range(N_KV_HEADS):
        ot = acc_ref[kh] / jnp.sum(ls[kh], axis=0, keepdims=True)
        for g in range(GROUP):
            h = kh * GROUP + g
            o_ref[:, h * HEAD_DIM:(h + 1) * HEAD_DIM] = ot[:, g * QB:(g + 1) * QB].T.astype(o_ref.dtype)


def _sparse_attention(iq, iw, q, ika, k, vt, B, S):
    N = B * S
    nb = S // Q_BLOCK
    n_sel = min(IDX_TOPK_MAX, S // 4)
    kvw = N_KV_HEADS * HEAD_DIM
    nchunks = S // KEY_CHUNK
    gw = GROUP * Q_BLOCK
    qspec = lambda w: pl.BlockSpec((Q_BLOCK, w), lambda b, i: (b * nb + i, 0))
    kspec = lambda w: pl.BlockSpec((S, w), lambda b, i: (b, 0))
    return pl.pallas_call(
        functools.partial(_attn_kernel, n_sel),
        grid=(B, nb),
        in_specs=[qspec(IDX_WIDTH), qspec(LANES), qspec(ATTN_WIDTH), kspec(LANES), kspec(kvw),
                  pl.BlockSpec((nchunks, kvw, KEY_CHUNK), lambda b, i: (b, 0, 0))],
        out_specs=qspec(ATTN_WIDTH),
        out_shape=jax.ShapeDtypeStruct((N, ATTN_WIDTH), BF16),
        scratch_shapes=[pltpu.VMEM((nchunks, KEY_CHUNK, Q_BLOCK), F32),
                        pltpu.VMEM((nchunks, KEY_CHUNK, Q_BLOCK), F32),
                        pltpu.VMEM((LANES, IDX_HEADS * Q_BLOCK), BF16),
                        pltpu.VMEM((LANES, Q_BLOCK), F32),
                        pltpu.VMEM((HEAD_DIM, N_HEADS * Q_BLOCK), BF16),
                        pltpu.VMEM((8, Q_BLOCK), I32),
                        pltpu.VMEM((nchunks, KEY_CHUNK, N_KV_HEADS * gw), F32),
                        pltpu.VMEM((N_KV_HEADS, HEAD_DIM, gw), F32)],
        compiler_params=_params(("arbitrary", "arbitrary")),
        name="dsa_attention",
    )(iq, iw, q, ika, k, vt)


def _split_bf16(a):
    hi = a.astype(BF16)
    lo = (a - hi.astype(F32)).astype(BF16)
    return hi, lo


def _mix_kernel(yc_ref, ao_ref, gate_ref, x_ref, g1_ref, wao_ref, wo_ref, n2_ref, sc2_ref, sh2_ref,
                wrh_ref, wrl_ref, br_ref, x1_ref, h2_ref, idx_ref, prob_ref):
    ya = jnp.dot(ao_ref[...], wao_ref[...], preferred_element_type=F32)
    mixed = yc_ref[...].astype(F32) + gate_ref[...].astype(F32) * ya
    z = jnp.dot(mixed.astype(BF16), wo_ref[...], preferred_element_type=F32)
    x1 = x_ref[...] + g1_ref[0] * z
    x1_ref[...] = x1
    ms = jnp.mean(x1 * x1, axis=-1, keepdims=True)
    h2 = x1 * lax.rsqrt(ms + EPS) * n2_ref[...] * (1.0 + sc2_ref[0]) + sh2_ref[0]
    h2_ref[...] = h2.astype(h2_ref.dtype)
    hh, hl = _split_bf16(h2)
    logits = (jnp.dot(hh, wrh_ref[...], preferred_element_type=F32)
              + jnp.dot(hh, wrl_ref[...], preferred_element_type=F32)
              + jnp.dot(hl, wrh_ref[...], preferred_element_type=F32)) + br_ref[...]
    lane = lax.broadcasted_iota(I32, logits.shape, 1)
    work = jnp.where(lane < N_EXPERTS, logits, -jnp.inf)
    vals = []
    idx_out = jnp.zeros(logits.shape, I32)
    for k in range(TOP_K):
        mv = jnp.max(work, axis=-1, keepdims=True)
        mi = jnp.min(jnp.where(work == mv, lane, LANES), axis=-1, keepdims=True)
        idx_out = jnp.where(lane == k, mi, idx_out)
        work = jnp.where(lane == mi, -jnp.inf, work)
        vals.append(mv)
    es = [jnp.exp(vk - vals[0]) for vk in vals]
    den = es[0]
    for e in es[1:]:
        den = den + e
    prob_out = jnp.zeros(logits.shape, F32)
    for k in range(TOP_K):
        prob_out = jnp.where(lane == k, es[k] / den, prob_out)
    idx_ref[...] = idx_out
    prob_ref[...] = prob_out


def _mix(yc, ao, gates, x, g1, w_attn_out, w_out, norm2_g, sc2, sh2, w_router, b_router, B, S):
    N, D = x.shape
    tm = _tile(S, 512)
    nS = S // tm
    wr = jnp.zeros((D, LANES), F32).at[:, :N_EXPERTS].set(w_router)
    wrh, wrl = _split_bf16(wr)
    br = jnp.zeros((1, LANES), F32).at[0, :N_EXPERTS].set(b_router)
    row = lambda w, c=0: pl.BlockSpec((tm, w), lambda i: (i, c))
    full = lambda a: pl.BlockSpec(a.shape, lambda i: (0,) * a.ndim)
    per_b = pl.BlockSpec((1, 1, D), lambda i: (i // nS, 0, 0))
    wao = w_attn_out.astype(BF16)
    wo = w_out.astype(BF16)
    n2 = norm2_g.reshape(1, D)
    return pl.pallas_call(
        _mix_kernel,
        grid=(N // tm,),
        in_specs=[row(D), row(ATTN_WIDTH), row(D, 1), row(D), per_b, full(wao), full(wo),
                  full(n2), per_b, per_b, full(wrh), full(wrl), full(br)],
        out_specs=[row(D), row(D), row(LANES), row(LANES)],
        out_shape=[jax.ShapeDtypeStruct((N, D), F32), jax.ShapeDtypeStruct((N, D), BF16),
                   jax.ShapeDtypeStruct((N, LANES), I32), jax.ShapeDtypeStruct((N, LANES), F32)],
        compiler_params=_params(("arbitrary",)),
        name="mix_router",
    )(yc, ao, gates, x, g1.reshape(B, 1, D), wao, wo, n2, sc2.reshape(B, 1, D),
      sh2.reshape(B, 1, D), wrh, wrl, br)


def _moe_kernel(blk_ref, exp_ref, nsub_ref, x_ref, wg_ref, bg_ref, wu_ref, bu_ref,
                wd_ref, bd_ref, o_ref, wg_s, wu_s, wd_s, acc_ref):
    w = pl.program_id(0)
    f = pl.program_id(1)
    nf = pl.num_programs(1)
    nsub = nsub_ref[w]
    n_sub_max = o_ref.shape[0] // MOE_SUB

    def zero_rows(s, carry):
        r0 = pl.multiple_of(s * MOE_SUB, MOE_SUB)
        o_ref[pl.ds(r0, MOE_SUB), :] = jnp.zeros((MOE_SUB, o_ref.shape[1]), o_ref.dtype)
        return carry

    @pl.when((nsub == 0) & (f == 0))
    def _():
        lax.fori_loop(0, n_sub_max, zero_rows, 0)

    @pl.when(nsub > 0)
    def _():
        wg_s[...] = wg_ref[0].astype(BF16)
        wu_s[...] = wu_ref[0].astype(BF16)
        wd_s[...] = wd_ref[0].astype(BF16)

        @pl.when(f == 0)
        def _():
            def zero(s, carry):
                r0 = pl.multiple_of(s * MOE_SUB, MOE_SUB)
                acc_ref[pl.ds(r0, MOE_SUB), :] = jnp.zeros((MOE_SUB, acc_ref.shape[1]), F32)
                return carry
            lax.fori_loop(0, nsub, zero, 0)

        def mlp_rows(r0, rows):
            xs = x_ref[pl.ds(r0, rows), :]
            g = jnp.dot(xs, wg_s[...], preferred_element_type=F32) + bg_ref[0]
            u = jnp.dot(xs, wu_s[...], preferred_element_type=F32) + bu_ref[0]
            g = jnp.minimum(g, SWIGLU_LIMIT)
            u = jnp.clip(u, -SWIGLU_LIMIT, SWIGLU_LIMIT)
            act = g * jax.nn.sigmoid(SWIGLU_ALPHA * g) * (u + 1.0)
            acc_ref[pl.ds(r0, rows), :] += jnp.dot(act.astype(BF16), wd_s[...],
                                                   preferred_element_type=F32)

        def pair(s, carry):
            mlp_rows(pl.multiple_of(s * (2 * MOE_SUB), 2 * MOE_SUB), 2 * MOE_SUB)
            return carry
        lax.fori_loop(0, nsub // 2, pair, 0)

        @pl.when(nsub % 2 == 1)
        def _():
            mlp_rows(pl.multiple_of((nsub - 1) * MOE_SUB, MOE_SUB), MOE_SUB)

        @pl.when(f == nf - 1)
        def _():
            def fin(s, carry):
                r0 = pl.multiple_of(s * MOE_SUB, MOE_SUB)
                y = acc_ref[pl.ds(r0, MOE_SUB), :] + bd_ref[0]
                o_ref[pl.ds(r0, MOE_SUB), :] = y.astype(o_ref.dtype)
                return carry
            lax.fori_loop(0, nsub, fin, 0)
            lax.fori_loop(nsub, n_sub_max, zero_rows, 0)


def _moe_experts(xs, item_blk, item_exp, item_nsub, w_gate, b_gate, w_up, b_up,
                 w_down, b_down):
    P, D = xs.shape
    E, _, F = w_gate.shape
    n_items = item_blk.shape[0]
    tf = _tile(F, MOE_FCHUNK)
    nf = F // tf

    def fidx(f, ns):
        return jnp.where(ns > 0, f, nf - 1)

    grid_spec = pltpu.PrefetchScalarGridSpec(
        num_scalar_prefetch=3,
        grid=(n_items, nf),
        in_specs=[
            pl.BlockSpec((MOE_TILE, D), lambda w, f, blk, ex, ns: (blk[w], 0)),
            pl.BlockSpec((1, D, tf), lambda w, f, blk, ex, ns: (ex[w], 0, fidx(f, ns[w]))),
            pl.BlockSpec((1, 1, tf), lambda w, f, blk, ex, ns: (ex[w], 0, fidx(f, ns[w]))),
            pl.BlockSpec((1, D, tf), lambda w, f, blk, ex, ns: (ex[w], 0, fidx(f, ns[w]))),
            pl.BlockSpec((1, 1, tf), lambda w, f, blk, ex, ns: (ex[w], 0, fidx(f, ns[w]))),
            pl.BlockSpec((1, tf, D), lambda w, f, blk, ex, ns: (ex[w], fidx(f, ns[w]), 0)),
            pl.BlockSpec((1, 1, D), lambda w, f, blk, ex, ns: (ex[w], 0, 0)),
        ],
        out_specs=pl.BlockSpec((MOE_TILE, D), lambda w, f, blk, ex, ns: (w, 0)),
        scratch_shapes=[pltpu.VMEM((D, tf), BF16), pltpu.VMEM((D, tf), BF16),
                        pltpu.VMEM((tf, D), BF16), pltpu.VMEM((MOE_TILE, D), F32)],
    )
    return pl.pallas_call(
        _moe_kernel,
        grid_spec=grid_spec,
        out_shape=jax.ShapeDtypeStruct((P, D), BF16),
        compiler_params=_params(("arbitrary", "arbitrary")),
        name="moe_experts",
    )(item_blk, item_exp, item_nsub, xs, w_gate, b_gate.reshape(E, 1, F), w_up,
      b_up.reshape(E, 1, F), w_down, b_down.reshape(E, 1, D))


def _route(top_idx):
    N = top_idx.shape[0]
    A = N * TOP_K
    n_items = A // MOE_TILE + N_EXPERTS
    P = n_items * MOE_TILE
    flat_e = top_idx.reshape(A)
    order = jnp.argsort(flat_e).astype(I32)
    pos = jnp.argsort(order).astype(I32)
    eids = jnp.arange(N_EXPERTS, dtype=I32)
    counts = jnp.sum((flat_e[:, None] == eids[None, :]).astype(I32), axis=0)
    grp_start = jnp.cumsum(counts) - counts
    n_it = (counts + MOE_TILE - 1) // MOE_TILE
    it_end = jnp.cumsum(n_it)
    it_start = it_end - n_it
    per_item = (counts + jnp.maximum(n_it, 1) - 1) // jnp.maximum(n_it, 1)
    per_item = jnp.maximum((per_item + MOE_SUB - 1) // MOE_SUB * MOE_SUB, MOE_SUB)
    n_used = it_end[-1]
    items = jnp.arange(n_items, dtype=I32)
    last = jnp.maximum(n_used - 1, 0)
    item_blk = jnp.minimum(items, last)
    item_exp = jnp.minimum(jnp.sum((item_blk[:, None] >= it_end[None, :]).astype(I32), axis=1),
                           N_EXPERTS - 1)
    k_in = item_blk - it_start[item_exp]
    first = grp_start[item_exp] + k_in * per_item[item_exp]
    nvalid = jnp.clip(counts[item_exp] - k_in * per_item[item_exp], 0, per_item[item_exp])
    nvalid = jnp.where(items < n_used, nvalid, 0)
    item_nsub = ((nvalid + MOE_SUB - 1) // MOE_SUB).astype(I32)
    t = jnp.arange(MOE_TILE, dtype=I32)
    valid = t[None, :] < nvalid[:, None]
    src = order[jnp.clip(first[:, None] + t[None, :], 0, A - 1)]
    rows = jnp.arange(P, dtype=I32).reshape(n_items, MOE_TILE)
    row_tok = jnp.where(valid, src // TOP_K, rows % N).reshape(P).astype(I32)
    off = pos - grp_start[flat_e]
    kk = off // per_item[flat_e]
    dest = (it_start[flat_e] + kk) * MOE_TILE + off - kk * per_item[flat_e]
    return row_tok, dest.astype(I32), item_blk, item_exp, item_nsub


def _final_kernel(x1_ref, y_ref, p_ref, g2_ref, o_ref):
    p = p_ref[...]
    y = y_ref[0].astype(F32) * p[:, 0:1]
    for k in range(1, TOP_K):
        y = y + y_ref[k].astype(F32) * p[:, k:k + 1]
    o_ref[...] = x1_ref[...] + g2_ref[0] * y


def _final(x1, yk, probs, g2, B, S):
    N, D = x1.shape
    tm = _tile(S, 512)
    nS = S // tm
    return pl.pallas_call(
        _final_kernel,
        grid=(N // tm,),
        in_specs=[pl.BlockSpec((tm, D), lambda i: (i, 0)),
                  pl.BlockSpec((TOP_K, tm, D), lambda i: (0, i, 0)),
                  pl.BlockSpec((tm, LANES), lambda i: (i, 0)),
                  pl.BlockSpec((1, 1, D), lambda i: (i // nS, 0, 0))],
        out_specs=pl.BlockSpec((tm, D), lambda i: (i, 0)),
        out_shape=jax.ShapeDtypeStruct((N, D), F32),
        compiler_params=_params(("arbitrary",)),
        name="moe_combine",
    )(x1, yk, probs, g2.reshape(B, 1, D))


def _layer(x, c, w_ada, b_ada, norm1_g, w_in, conv_w, conv_b, conv_ln_g, conv_ln_b, w_conv_out,
           q_norm_g, k_norm_g, w_attn_out, w_out, norm2_g, w_router, b_router, w_gate, b_gate,
           w_up, b_up, w_down, b_down):
    B, S, D = x.shape
    N = B * S
    C = conv_w.shape[1]
    kvw = N_KV_HEADS * HEAD_DIM

    mod = _ada(c, w_ada, b_ada)
    sh1, sc1, g1, sh2, sc2, g2 = [mod[:, j * D:(j + 1) * D] for j in range(6)]

    h = _normmod(x, norm1_g, sc1, sh1).reshape(N, D)

    o = 0
    w_conv_in = w_in[:, o:o + 2 * C]; o += 2 * C
    w_q = w_in[:, o:o + ATTN_WIDTH]; o += ATTN_WIDTH
    w_kv_small = w_in[:, o:o + 2 * kvw + IDX_WIDTH + IDX_DIM + IDX_HEADS]
    w_k_v = w_in[:, o:o + 2 * kvw]; o += 2 * kvw
    w_iq = w_in[:, o:o + IDX_WIDTH]; o += IDX_WIDTH
    w_ik_iw = w_in[:, o:o + IDX_DIM + IDX_HEADS]; o += IDX_DIM + IDX_HEADS
    w_gates = w_in[:, o:o + 2 * D]
    del w_kv_small

    tm = _tile(S, 1024)
    nS = S // tm
    cos128, sin128 = _rope_tables(S, HEAD_DIM)
    cos64, sin64 = _rope_tables(S, IDX_DIM)
    tab = pl.BlockSpec((tm, LANES), lambda i, j: (i % nS, 0))
    vec = pl.BlockSpec((1, LANES), lambda i, j: (0, 0))

    tn = _tile(2 * C, 1024)
    hc = tn // 2
    wv = w_conv_in[:, :C].reshape(D, C // hc, hc)
    wg = w_conv_in[:, C:].reshape(D, C // hc, hc)
    w_glu = jnp.concatenate([wv, wg], axis=2).reshape(D, 2 * C).astype(BF16)
    u = _proj(_glu_kernel, h, w_glu, tn, [], [],
              jax.ShapeDtypeStruct((N, C), BF16),
              pl.BlockSpec((tm, hc), lambda i, j: (i, j)), tm, "proj_conv_glu")

    tq = _tile(ATTN_WIDTH, 512)
    q = _proj(_q_kernel, h, w_q.astype(BF16), tq, [cos128, sin128, q_norm_g.reshape(1, HEAD_DIM)],
              [tab, tab, vec], jax.ShapeDtypeStruct((N, ATTN_WIDTH), BF16),
              pl.BlockSpec((tm, tq), lambda i, j: (i, j)), tm, "proj_q")

    pad = LANES - IDX_DIM - IDX_HEADS
    w_kvs = jnp.concatenate([w_k_v, w_ik_iw, jnp.zeros((D, pad), F32)], axis=1).astype(BF16)
    nk = w_kvs.shape[1]
    blk = lambda w: pl.BlockSpec((tm, w), lambda i, j: (i, 0))
    cpt = tm // KEY_CHUNK
    k, vt, ika, iw = _proj(
        _kv_kernel, h, w_kvs, nk,
        [cos128, sin128, cos64, sin64, k_norm_g.reshape(1, HEAD_DIM)], [tab, tab, tab, tab, vec],
        [jax.ShapeDtypeStruct((N, kvw), BF16), jax.ShapeDtypeStruct((N // KEY_CHUNK, kvw, KEY_CHUNK), BF16),
         jax.ShapeDtypeStruct((N, LANES), BF16), jax.ShapeDtypeStruct((N, LANES), F32)],
        [blk(kvw), pl.BlockSpec((cpt, kvw, KEY_CHUNK), lambda i, j: (i, 0, 0)), blk(LANES), blk(LANES)],
        tm, "proj_kv_indexer_key")

    ti = _tile(IDX_WIDTH, 512)
    iq = _proj(_iq_kernel, h, w_iq.astype(BF16), ti, [cos64, sin64], [tab, tab],
               jax.ShapeDtypeStruct((N, IDX_WIDTH), BF16),
               pl.BlockSpec((tm, ti), lambda i, j: (i, j)), tm, "proj_iq")

    tg = _tile(2 * D, 1024)
    gates = _proj(_sigmoid_kernel, h, w_gates.astype(BF16), tg, [], [],
                  jax.ShapeDtypeStruct((N, 2 * D), BF16),
                  pl.BlockSpec((tm, tg), lambda i, j: (i, j)), tm, "proj_gates")

    yc = _conv_branch(u.reshape(B, S, C), conv_w, conv_b, conv_ln_g, conv_ln_b, w_conv_out,
                      gates.reshape(B, S, 2 * D)).reshape(N, D)
    ao = _sparse_attention(iq, iw, q, ika, k, vt, B, S)

    x1, h2, top_idx, probs = _mix(yc, ao, gates, x.reshape(N, D), g1, w_attn_out, w_out, norm2_g,
                                  sc2, sh2, w_router, b_router, B, S)

    row_tok, dest, item_blk, item_exp, item_nsub = _route(top_idx[:, :TOP_K])
    xs = h2.at[row_tok].get(mode="promise_in_bounds")
    ys = _moe_experts(xs, item_blk, item_exp, item_nsub, w_gate, b_gate, w_up, b_up, w_down,
                      b_down)
    yk = ys.at[dest.reshape(N, TOP_K).T].get(mode="promise_in_bounds")
    out = _final(x1, yk, probs, g2, B, S)
    return out.reshape(B, S, D)


def kernel(x, c, w_ada, b_ada, norm1_g, w_in, conv_w, conv_b, conv_ln_g, conv_ln_b, w_conv_out,
           q_norm_g, k_norm_g, w_attn_out, w_out, norm2_g, w_router, b_router, w_gate, b_gate,
           w_up, b_up, w_down, b_down):
    for l in range(w_ada.shape[0]):
        x = _layer(x, c, w_ada[l], b_ada[l], norm1_g[l], w_in[l], conv_w[l], conv_b[l],
                   conv_ln_g[l], conv_ln_b[l], w_conv_out[l], q_norm_g[l], k_norm_g[l],
                   w_attn_out[l], w_out[l], norm2_g[l], w_router[l], b_router[l], w_gate[l],
                   b_gate[l], w_up[l], b_up[l], w_down[l], b_down[l])
    return x
```

```python
import functools

import jax
import jax.numpy as jnp
from jax import lax
from jax.experimental import pallas as pl
from jax.experimental.pallas import tpu as pltpu

F32 = jnp.float32
BF16 = jnp.bfloat16
I32 = jnp.int32

EPS = 1e-6
CONV_TAPS = 31
N_HEADS = 8
N_KV_HEADS = 2
HEAD_DIM = 128
GROUP = N_HEADS // N_KV_HEADS
ATTN_WIDTH = N_HEADS * HEAD_DIM
ROPE_THETA = 10000.0
Q_BLOCK = 128
IDX_HEADS = 16
IDX_DIM = 64
IDX_WIDTH = IDX_HEADS * IDX_DIM
IDX_TOPK_MAX = 256
N_EXPERTS = 32
TOP_K = 4
SWIGLU_ALPHA = 1.702
SWIGLU_LIMIT = 7.0

LANES = 128
CONV_HALO = 32
PROJ_SLAB = 256
KEY_CHUNK = 256
MOE_TILE = 1024
MOE_SUB = 256
MOE_FCHUNK = 512
NEG_BIG = -1e30
INT_MIN = -2 ** 31
VMEM_LIMIT = 60 * 1024 * 1024


def _tile(dim, pref):
    if dim <= pref:
        return dim
    t = pref - pref % LANES
    while t >= LANES:
        if dim % t == 0:
            return t
        t -= LANES
    return dim


def _params(sem):
    return pltpu.CompilerParams(dimension_semantics=sem, vmem_limit_bytes=VMEM_LIMIT)


def _ada_kernel(c_ref, w_ref, b_ref, o_ref):
    c = c_ref[...]
    sc = c * jax.nn.sigmoid(c)
    o_ref[...] = jnp.dot(sc.astype(BF16), w_ref[...].astype(BF16),
                         preferred_element_type=F32) + b_ref[...]


def _ada(c, w_ada, b_ada):
    B, D = c.shape
    n = w_ada.shape[1]
    rows = 8
    cp = jnp.zeros((rows, D), F32).at[:B].set(c)
    tn = _tile(n, 1024)
    out = pl.pallas_call(
        _ada_kernel,
        grid=(n // tn,),
        in_specs=[pl.BlockSpec((rows, D), lambda j: (0, 0)),
                  pl.BlockSpec((D, tn), lambda j: (0, j)),
                  pl.BlockSpec((1, tn), lambda j: (0, j))],
        out_specs=pl.BlockSpec((rows, tn), lambda j: (0, j)),
        out_shape=jax.ShapeDtypeStruct((rows, n), F32),
        compiler_params=_params(("arbitrary",)),
        name="ada_mod",
    )(cp, w_ada, b_ada.reshape(1, n))
    return out[:B]


def _normmod_kernel(x_ref, g_ref, sc_ref, sh_ref, o_ref):
    x = x_ref[0]
    ms = jnp.mean(x * x, axis=-1, keepdims=True)
    y = x * lax.rsqrt(ms + EPS) * g_ref[...]
    o_ref[0] = (y * (1.0 + sc_ref[0]) + sh_ref[0]).astype(o_ref.dtype)


def _normmod(x, g, sc, sh):
    B, S, D = x.shape
    ts = _tile(S, 512)
    return pl.pallas_call(
        _normmod_kernel,
        grid=(B, S // ts),
        in_specs=[pl.BlockSpec((1, ts, D), lambda b, i: (b, i, 0)),
                  pl.BlockSpec((1, D), lambda b, i: (0, 0)),
                  pl.BlockSpec((1, 1, D), lambda b, i: (b, 0, 0)),
                  pl.BlockSpec((1, 1, D), lambda b, i: (b, 0, 0))],
        out_specs=pl.BlockSpec((1, ts, D), lambda b, i: (b, i, 0)),
        out_shape=jax.ShapeDtypeStruct((B, S, D), BF16),
        compiler_params=_params(("arbitrary", "arbitrary")),
        name="norm1_mod",
    )(x, g.reshape(1, D), sc.reshape(B, 1, D), sh.reshape(B, 1, D))


def _rope_tables(S, d):
    inv = ROPE_THETA ** (-jnp.arange(0, d, 2, dtype=F32) / d)
    ang = jnp.arange(S, dtype=F32)[:, None] * inv[None, :]
    cos = jnp.cos(ang)
    sin = jnp.sin(ang)
    reps = LANES // d
    cos_t = jnp.tile(jnp.concatenate([cos, cos], axis=1), (1, reps))
    sin_t = jnp.tile(jnp.concatenate([-sin, sin], axis=1), (1, reps))
    return cos_t, sin_t


def _rope128(y, cos, sin):
    return y * cos + pltpu.roll(y, 64, axis=1) * sin


def _rope64(y, cos, sin):
    lane = lax.broadcasted_iota(I32, y.shape, 1)
    rot = jnp.where((lane % 64) < 32, pltpu.roll(y, 96, axis=1), pltpu.roll(y, 32, axis=1))
    return y * cos + rot * sin


def _glu_kernel(x_ref, w_ref, o_ref):
    acc = jnp.dot(x_ref[...], w_ref[...], preferred_element_type=F32)
    half = acc.shape[1] // 2
    o_ref[...] = (acc[:, :half] * jax.nn.sigmoid(acc[:, half:])).astype(o_ref.dtype)


def _row_slabs(rows):
    step = min(rows, PROJ_SLAB)
    return [(r, step) for r in range(0, rows, step)]


def _q_kernel(x_ref, w_ref, cos_ref, sin_ref, g_ref, o_ref):
    g = g_ref[...]
    for r0, n in _row_slabs(x_ref.shape[0]):
        acc = jnp.dot(x_ref[r0:r0 + n, :], w_ref[...], preferred_element_type=F32)
        cos = cos_ref[r0:r0 + n, :]
        sin = sin_ref[r0:r0 + n, :]
        for h in range(acc.shape[1] // HEAD_DIM):
            xh = acc[:, h * HEAD_DIM:(h + 1) * HEAD_DIM]
            ms = jnp.mean(xh * xh, axis=-1, keepdims=True)
            y = xh * lax.rsqrt(ms + EPS) * g
            o_ref[r0:r0 + n, h * HEAD_DIM:(h + 1) * HEAD_DIM] = (
                _rope128(y, cos, sin) * (HEAD_DIM ** -0.5)).astype(o_ref.dtype)


def _kv_kernel(x_ref, w_ref, cos_ref, sin_ref, cos64_ref, sin64_ref, g_ref,
               k_ref, vt_ref, ika_ref, iw_ref):
    kvw = N_KV_HEADS * HEAD_DIM
    g = g_ref[...]
    for c in range(vt_ref.shape[0]):
        r0, n = c * KEY_CHUNK, KEY_CHUNK
        acc = jnp.dot(x_ref[r0:r0 + n, :], w_ref[...], preferred_element_type=F32)
        cos = cos_ref[r0:r0 + n, :]
        sin = sin_ref[r0:r0 + n, :]
        for h in range(N_KV_HEADS):
            xh = acc[:, h * HEAD_DIM:(h + 1) * HEAD_DIM]
            ms = jnp.mean(xh * xh, axis=-1, keepdims=True)
            y = xh * lax.rsqrt(ms + EPS) * g
            k_ref[r0:r0 + n, h * HEAD_DIM:(h + 1) * HEAD_DIM] = _rope128(y, cos, sin).astype(k_ref.dtype)
        vt_ref[c] = acc[:, kvw:2 * kvw].T.astype(vt_ref.dtype)
        tail = acc[:, 2 * kvw:2 * kvw + LANES]
        lane = lax.broadcasted_iota(I32, tail.shape, 1)
        roped = _rope64(tail, cos64_ref[r0:r0 + n, :], sin64_ref[r0:r0 + n, :])
        ika_ref[r0:r0 + n, :] = jnp.where(lane < IDX_DIM, roped, 0.0).astype(ika_ref.dtype)
        iw = pltpu.roll(tail, LANES - IDX_DIM, axis=1)
        iw_ref[r0:r0 + n, :] = jnp.where(lane < IDX_HEADS, iw * (IDX_HEADS ** -0.5), 0.0)


def _iq_kernel(x_ref, w_ref, cos_ref, sin_ref, o_ref):
    for r0, n in _row_slabs(x_ref.shape[0]):
        acc = jnp.dot(x_ref[r0:r0 + n, :], w_ref[...], preferred_element_type=F32)
        cos = cos_ref[r0:r0 + n, :]
        sin = sin_ref[r0:r0 + n, :]
        for p in range(acc.shape[1] // LANES):
            y = acc[:, p * LANES:(p + 1) * LANES]
            o_ref[r0:r0 + n, p * LANES:(p + 1) * LANES] = (
                _rope64(y, cos, sin) * (IDX_DIM ** -0.5)).astype(o_ref.dtype)


def _sigmoid_kernel(x_ref, w_ref, o_ref):
    acc = jnp.dot(x_ref[...], w_ref[...], preferred_element_type=F32)
    o_ref[...] = jax.nn.sigmoid(acc).astype(o_ref.dtype)


def _proj(kernel, h, w, tn, aux, aux_specs, out_shapes, out_specs, tm, name):
    N, D = h.shape
    n = w.shape[1]
    return pl.pallas_call(
        kernel,
        grid=(N // tm, n // tn),
        in_specs=[pl.BlockSpec((tm, D), lambda i, j: (i, 0)),
                  pl.BlockSpec((D, tn), lambda i, j: (0, j))] + aux_specs,
        out_specs=out_specs,
        out_shape=out_shapes,
        compiler_params=_params(("arbitrary", "arbitrary")),
        name=name,
    )(h, w, *aux)


def _conv_kernel(u_ref, halo_ref, cw_ref, cb_ref, lg_ref, lb_ref, wo_ref, gate_ref, o_ref,
                 buf_ref, sh_ref, conv_ref):
    i = pl.program_id(1)
    ts = u_ref.shape[1]
    SUBL = 8
    halo = halo_ref[0].astype(F32)
    buf_ref[0:CONV_HALO, :] = jnp.where(i == 0, 0.0, halo)
    buf_ref[CONV_HALO:, :] = u_ref[0].astype(F32)
    for b in range(1, SUBL):
        sh_ref[b - 1] = buf_ref[b:b + sh_ref.shape[1], :]
    rc = 32
    first = CONV_HALO - (CONV_TAPS - 1)

    def chunk(c, carry):
        r0 = pl.multiple_of(c * rc, rc)
        acc = jnp.zeros((rc, buf_ref.shape[1]), F32)
        for t in range(CONV_TAPS):
            a, b = divmod(first + t, SUBL)
            if b == 0:
                rows = buf_ref[pl.ds(r0 + a * SUBL, rc), :]
            else:
                rows = sh_ref[b - 1, pl.ds(r0 + a * SUBL, rc), :]
            acc = acc + cw_ref[t:t + 1, :] * rows
        conv_ref[pl.ds(r0, rc), :] = acc
        return carry

    lax.fori_loop(0, ts // rc, chunk, 0)
    u = conv_ref[...] + cb_ref[...]
    mu = jnp.mean(u, axis=-1, keepdims=True)
    uc = u - mu
    var = jnp.mean(uc * uc, axis=-1, keepdims=True)
    y = uc * lax.rsqrt(var + EPS) * lg_ref[...] + lb_ref[...]
    y = y * jax.nn.sigmoid(y)
    z = jnp.dot(y.astype(BF16), wo_ref[...], preferred_element_type=F32)
    o_ref[0] = (z * gate_ref[0].astype(F32)).astype(o_ref.dtype)


def _conv_branch(u, conv_w, conv_b, ln_g, ln_b, w_conv_out, gates):
    B, S, C = u.shape
    D = w_conv_out.shape[1]
    ts = _tile(S, 256)
    hb = ts // CONV_HALO
    cw = jnp.zeros((CONV_HALO, C), F32).at[:CONV_TAPS].set(conv_w)
    return pl.pallas_call(
        _conv_kernel,
        grid=(B, S // ts),
        in_specs=[pl.BlockSpec((1, ts, C), lambda b, i: (b, i, 0)),
                  pl.BlockSpec((1, CONV_HALO, C), lambda b, i: (b, jnp.maximum(i * hb - 1, 0), 0)),
                  pl.BlockSpec((CONV_HALO, C), lambda b, i: (0, 0)),
                  pl.BlockSpec((1, C), lambda b, i: (0, 0)),
                  pl.BlockSpec((1, C), lambda b, i: (0, 0)),
                  pl.BlockSpec((1, C), lambda b, i: (0, 0)),
                  pl.BlockSpec((C, D), lambda b, i: (0, 0)),
                  pl.BlockSpec((1, ts, D), lambda b, i: (b, i, 0))],
        out_specs=pl.BlockSpec((1, ts, D), lambda b, i: (b, i, 0)),
        out_shape=jax.ShapeDtypeStruct((B, S, D), BF16),
        scratch_shapes=[pltpu.VMEM((ts + CONV_HALO, C), F32),
                        pltpu.VMEM((7, ts + CONV_HALO - 8, C), F32),
                        pltpu.VMEM((ts, C), F32)],
        compiler_params=_params(("arbitrary", "arbitrary")),
        name="conv_branch",
    )(u, u, cw, conv_b.reshape(1, C), ln_g.reshape(1, C), ln_b.reshape(1, C),
      w_conv_out.astype(BF16), gates)


def _attn_kernel(n_sel, iq_ref, iw_ref, q_ref, ika_ref, k_ref, vt_ref, o_ref,
                 key_ref, bias_ref, iqt_ref, wt_ref, qt_ref, bound_ref, s_ref, acc_ref):
    i = pl.program_id(1)
    QB = Q_BLOCK
    CK = KEY_CHUNK
    GW = GROUP * QB
    SUBL = 8
    nck = (i * QB + QB + CK - 1) // CK
    q_pos = i * QB + lax.broadcasted_iota(I32, (CK, QB), 1)
    krow = lax.broadcasted_iota(I32, (CK, QB), 0)
    q_pos8 = q_pos[0:SUBL]
    row8 = krow[0:SUBL]

    iq = iq_ref[...].astype(F32)
    iqt_ref[IDX_DIM:, :] = jnp.zeros((LANES - IDX_DIM, IDX_HEADS * QB), BF16)
    for p in range(IDX_HEADS // 2):
        t = iq[:, p * LANES:(p + 1) * LANES].T
        iqt_ref[0:IDX_DIM, (2 * p) * QB:(2 * p + 1) * QB] = t[0:IDX_DIM].astype(BF16)
        iqt_ref[0:IDX_DIM, (2 * p + 1) * QB:(2 * p + 2) * QB] = t[IDX_DIM:].astype(BF16)
    wt_ref[...] = iw_ref[...].T
    qf = q_ref[...].astype(F32)
    for h in range(N_HEADS):
        qt_ref[:, h * QB:(h + 1) * QB] = qf[:, h * HEAD_DIM:(h + 1) * HEAD_DIM].T.astype(BF16)

    def score_chunk(c, carry):
        k0 = pl.multiple_of(c * CK, CK)
        ka = ika_ref[pl.ds(k0, CK), :]
        acc = jnp.zeros((CK, QB), F32)
        for hg in range(IDX_HEADS // 4):
            st = jnp.dot(ka, iqt_ref[:, hg * 4 * QB:(hg + 1) * 4 * QB], preferred_element_type=F32)
            for j in range(4):
                h = hg * 4 + j
                acc = acc + jnp.maximum(st[:, j * QB:(j + 1) * QB], 0.0) * wt_ref[h:h + 1, :]
        key_ref[c] = jnp.where(k0 + krow <= q_pos, acc, -jnp.inf)
        return carry

    lax.fori_loop(0, nck, score_chunk, 0)

    def count(pred):
        def body(c, cnts):
            cnts = list(cnts)
            for r in range(CK // SUBL):
                sc = key_ref[c, r * SUBL:(r + 1) * SUBL, :]
                kidx = c * CK + r * SUBL + row8
                cnts[r % 4] = cnts[r % 4] + pred(sc, kidx).astype(I32)
            return tuple(cnts)
        z = jnp.zeros((SUBL, QB), I32)
        cnts = lax.fori_loop(0, nck, body, (z, z, z, z))
        tot = (cnts[0] + cnts[1]) + (cnts[2] + cnts[3])
        return jnp.broadcast_to(jnp.sum(tot, axis=0, keepdims=True), (SUBL, QB))

    def code_value(code):
        return pltpu.bitcast(code ^ ((code >> 31) & 0x7FFFFFFF), F32)

    zero = jnp.zeros((SUBL, QB), I32)
    zero_f = jnp.zeros((SUBL, QB), F32)
    code = jnp.where(count(lambda sc, kidx: sc >= zero_f) >= n_sel, zero, INT_MIN)

    def bit_step(b, code):
        cand = code | (1 << (30 - b))
        cand_f = code_value(cand)
        return jnp.where(count(lambda sc, kidx: sc >= cand_f) >= n_sel, cand, code)

    code = lax.fori_loop(0, 31, bit_step, code)
    thr = jnp.where(code == INT_MIN, -jnp.inf, code_value(code))

    n_gt = count(lambda sc, kidx: sc > thr)
    n_ge = count(lambda sc, kidx: sc >= thr)
    need = n_sel - n_gt
    bound_ref[...] = jnp.full((SUBL, QB), 2 ** 30, I32)

    @pl.when(jnp.max(jnp.where(code > INT_MIN, n_ge, 0)) > n_sel)
    def _():
        def idx_step(b, bound):
            cand = bound | (1 << (14 - b))
            n_tie = count(lambda sc, kidx: (sc == thr) & (kidx < cand))
            return jnp.where(n_tie <= need, cand, bound)
        bound_ref[...] = lax.fori_loop(0, 15, idx_step, zero)

    bound = bound_ref[...]

    def bias_chunk(c, carry):
        for r in range(CK // SUBL):
            sc = key_ref[c, r * SUBL:(r + 1) * SUBL, :]
            kidx = c * CK + r * SUBL + row8
            sel = ((sc > thr) | ((sc == thr) & (kidx < bound))) & (kidx <= q_pos8)
            bias_ref[c, r * SUBL:(r + 1) * SUBL, :] = jnp.where(sel, 0.0, NEG_BIG)
        return carry

    lax.fori_loop(0, nck, bias_chunk, 0)

    def logit_chunk(c, mx):
        k0 = pl.multiple_of(c * CK, CK)
        bias = jnp.concatenate([bias_ref[c]] * GROUP, axis=1)
        out = []
        for kh in range(N_KV_HEADS):
            kc = k_ref[pl.ds(k0, CK), kh * HEAD_DIM:(kh + 1) * HEAD_DIM]
            st = jnp.dot(kc, qt_ref[:, kh * GW:(kh + 1) * GW], preferred_element_type=F32) + bias
            s_ref[c, :, kh * GW:(kh + 1) * GW] = st
            m8 = mx[kh]
            for r in range(CK // SUBL):
                m8 = jnp.maximum(m8, st[r * SUBL:(r + 1) * SUBL, :])
            out.append(m8)
        return tuple(out)

    mx = lax.fori_loop(0, nck, logit_chunk,
                       tuple(jnp.full((SUBL, GW), NEG_BIG, F32) for _ in range(N_KV_HEADS)))
    m_fin = [jnp.max(m8, axis=0, keepdims=True) for m8 in mx]

    acc_ref[...] = jnp.zeros(acc_ref.shape, F32)

    def value_chunk(c, ls):
        out = []
        for kh in range(N_KV_HEADS):
            p = jnp.exp(s_ref[c, :, kh * GW:(kh + 1) * GW] - m_fin[kh])
            l8 = ls[kh]
            for r in range(CK // SUBL):
                l8 = l8 + p[r * SUBL:(r + 1) * SUBL, :]
            out.append(l8)
            vt = vt_ref[c, kh * HEAD_DIM:(kh + 1) * HEAD_DIM, :]
            acc_ref[kh] += jnp.dot(vt, p.astype(BF16), preferred_element_type=F32)
        return tuple(out)

    ls = lax.fori_loop(0, nck, value_chunk,
                       tuple(jnp.zeros((SUBL, GW), F32) for _ in range(N_KV_HEADS)))
    for kh in range(N_KV_HEADS):
        ot = acc_ref[kh] / jnp.sum(ls[kh], axis=0, keepdims=True)
        for g in range(GROUP):
            h = kh * GROUP + g
            o_ref[:, h * HEAD_DIM:(h + 1) * HEAD_DIM] = ot[:, g * QB:(g + 1) * QB].T.astype(o_ref.dtype)


def _sparse_attention(iq, iw, q, ika, k, vt, B, S):
    N = B * S
    nb = S // Q_BLOCK
    n_sel = min(IDX_TOPK_MAX, S // 4)
    kvw = N_KV_HEADS * HEAD_DIM
    nchunks = S // KEY_CHUNK
    gw = GROUP * Q_BLOCK
    qspec = lambda w: pl.BlockSpec((Q_BLOCK, w), lambda b, i: (b * nb + i, 0))
    kspec = lambda w: pl.BlockSpec((S, w), lambda b, i: (b, 0))
    return pl.pallas_call(
        functools.partial(_attn_kernel, n_sel),
        grid=(B, nb),
        in_specs=[qspec(IDX_WIDTH), qspec(LANES), qspec(ATTN_WIDTH), kspec(LANES), kspec(kvw),
                  pl.BlockSpec((nchunks, kvw, KEY_CHUNK), lambda b, i: (b, 0, 0))],
        out_specs=qspec(ATTN_WIDTH),
        out_shape=jax.ShapeDtypeStruct((N, ATTN_WIDTH), BF16),
        scratch_shapes=[pltpu.VMEM((nchunks, KEY_CHUNK, Q_BLOCK), F32),
                        pltpu.VMEM((nchunks, KEY_CHUNK, Q_BLOCK), F32),
                        pltpu.VMEM((LANES, IDX_HEADS * Q_BLOCK), BF16),
                        pltpu.VMEM((LANES, Q_BLOCK), F32),
                        pltpu.VMEM((HEAD_DIM, N_HEADS * Q_BLOCK), BF16),
                        pltpu.VMEM((8, Q_BLOCK), I32),
                        pltpu.VMEM((nchunks, KEY_CHUNK, N_KV_HEADS * gw), F32),
                        pltpu.VMEM((N_KV_HEADS, HEAD_DIM, gw), F32)],
        compiler_params=_params(("arbitrary", "arbitrary")),
        name="dsa_attention",
    )(iq, iw, q, ika, k, vt)


def _split_bf16(a):
    hi = a.astype(BF16)
    lo = (a - hi.astype(F32)).astype(BF16)
    return hi, lo


def _mix_kernel(yc_ref, ao_ref, gate_ref, x_ref, g1_ref, wao_ref, wo_ref, n2_ref, sc2_ref, sh2_ref,
                wrh_ref, wrl_ref, br_ref, x1_ref, h2_ref, idx_ref, prob_ref):
    for r0, n in _row_slabs(x_ref.shape[0]):
        rows = slice(r0, r0 + n)
        ya = jnp.dot(ao_ref[rows, :], wao_ref[...], preferred_element_type=F32)
        mixed = yc_ref[rows, :].astype(F32) + gate_ref[rows, :].astype(F32) * ya
        z = jnp.dot(mixed.astype(BF16), wo_ref[...], preferred_element_type=F32)
        x1 = x_ref[rows, :] + g1_ref[0] * z
        x1_ref[rows, :] = x1
        ms = jnp.mean(x1 * x1, axis=-1, keepdims=True)
        h2 = x1 * lax.rsqrt(ms + EPS) * n2_ref[...] * (1.0 + sc2_ref[0]) + sh2_ref[0]
        h2_ref[rows, :] = h2.astype(h2_ref.dtype)
        hh, hl = _split_bf16(h2)
        logits = (jnp.dot(hh, wrh_ref[...], preferred_element_type=F32)
                  + jnp.dot(hh, wrl_ref[...], preferred_element_type=F32)
                  + jnp.dot(hl, wrh_ref[...], preferred_element_type=F32)) + br_ref[...]
        lane = lax.broadcasted_iota(I32, logits.shape, 1)
        work = jnp.where(lane < N_EXPERTS, logits, -jnp.inf)
        vals = []
        idx_out = jnp.zeros(logits.shape, I32)
        for k in range(TOP_K):
            mv = jnp.max(work, axis=-1, keepdims=True)
            mi = jnp.min(jnp.where(work == mv, lane, LANES), axis=-1, keepdims=True)
            idx_out = jnp.where(lane == k, mi, idx_out)
            work = jnp.where(lane == mi, -jnp.inf, work)
            vals.append(mv)
        es = [jnp.exp(vk - vals[0]) for vk in vals]
        den = es[0]
        for e in es[1:]:
            den = den + e
        prob_out = jnp.zeros(logits.shape, F32)
        for k in range(TOP_K):
            prob_out = jnp.where(lane == k, es[k] / den, prob_out)
        idx_ref[rows, :] = idx_out
        prob_ref[rows, :] = prob_out


def _mix(yc, ao, gates, x, g1, w_attn_out, w_out, norm2_g, sc2, sh2, w_router, b_router, B, S):
    N, D = x.shape
    tm = _tile(S, 512)
    nS = S // tm
    wr = jnp.zeros((D, LANES), F32).at[:, :N_EXPERTS].set(w_router)
    wrh, wrl = _split_bf16(wr)
    br = jnp.zeros((1, LANES), F32).at[0, :N_EXPERTS].set(b_router)
    row = lambda w, c=0: pl.BlockSpec((tm, w), lambda i: (i, c))
    full = lambda a: pl.BlockSpec(a.shape, lambda i: (0,) * a.ndim)
    per_b = pl.BlockSpec((1, 1, D), lambda i: (i // nS, 0, 0))
    wao = w_attn_out.astype(BF16)
    wo = w_out.astype(BF16)
    n2 = norm2_g.reshape(1, D)
    return pl.pallas_call(
        _mix_kernel,
        grid=(N // tm,),
        in_specs=[row(D), row(ATTN_WIDTH), row(D, 1), row(D), per_b, full(wao), full(wo),
                  full(n2), per_b, per_b, full(wrh), full(wrl), full(br)],
        out_specs=[row(D), row(D), row(LANES), row(LANES)],
        out_shape=[jax.ShapeDtypeStruct((N, D), F32), jax.ShapeDtypeStruct((N, D), BF16),
                   jax.ShapeDtypeStruct((N, LANES), I32), jax.ShapeDtypeStruct((N, LANES), F32)],
        compiler_params=_params(("arbitrary",)),
        name="mix_router",
    )(yc, ao, gates, x, g1.reshape(B, 1, D), wao, wo, n2, sc2.reshape(B, 1, D),
      sh2.reshape(B, 1, D), wrh, wrl, br)


def _moe_kernel(blk_ref, exp_ref, nsub_ref, x_ref, wg_ref, bg_ref, wu_ref, bu_ref,
                wd_ref, bd_ref, o_ref, wg_s, wu_s, wd_s, acc_ref):
    w = pl.program_id(0)
    f = pl.program_id(1)
    nf = pl.num_programs(1)
    nsub = nsub_ref[w]
    n_sub_max = o_ref.shape[0] // MOE_SUB

    def zero_rows(s, carry):
        r0 = pl.multiple_of(s * MOE_SUB, MOE_SUB)
        o_ref[pl.ds(r0, MOE_SUB), :] = jnp.zeros((MOE_SUB, o_ref.shape[1]), o_ref.dtype)
        return carry

    @pl.when((nsub == 0) & (f == 0))
    def _():
        lax.fori_loop(0, n_sub_max, zero_rows, 0)

    @pl.when(nsub > 0)
    def _():
        @pl.when(f == 0)
        def _():
            def zero(s, carry):
                r0 = pl.multiple_of(s * MOE_SUB, MOE_SUB)
                acc_ref[pl.ds(r0, MOE_SUB), :] = jnp.zeros((MOE_SUB, acc_ref.shape[1]), F32)
                return carry
            lax.fori_loop(0, nsub, zero, 0)

        def mlp_rows(r0, rows):
            xs = x_ref[pl.ds(r0, rows), :]
            g = jnp.dot(xs, wg_s[...], preferred_element_type=F32) + bg_ref[0]
            u = jnp.dot(xs, wu_s[...], preferred_element_type=F32) + bu_ref[0]
            g = jnp.minimum(g, SWIGLU_LIMIT)
            u = jnp.clip(u, -SWIGLU_LIMIT, SWIGLU_LIMIT)
            act = g * jax.nn.sigmoid(SWIGLU_ALPHA * g) * (u + 1.0)
            acc_ref[pl.ds(r0, rows), :] += jnp.dot(act.astype(BF16), wd_s[...],
                                                   preferred_element_type=F32)

        xs0 = x_ref[0:MOE_SUB, :]
        tile = 2 * LANES
        g0 = jnp.zeros((MOE_SUB, wg_s.shape[1]), F32) + bg_ref[0]
        u0 = jnp.zeros((MOE_SUB, wu_s.shape[1]), F32) + bu_ref[0]
        for kt in range(xs0.shape[1] // tile):
            ks = slice(kt * tile, (kt + 1) * tile)
            wgt = wg_ref[0, ks, :].astype(BF16)
            wut = wu_ref[0, ks, :].astype(BF16)
            wg_s[ks, :] = wgt
            wu_s[ks, :] = wut
            g0 = g0 + jnp.dot(xs0[:, ks], wgt, preferred_element_type=F32)
            u0 = u0 + jnp.dot(xs0[:, ks], wut, preferred_element_type=F32)
        g0 = jnp.minimum(g0, SWIGLU_LIMIT)
        u0 = jnp.clip(u0, -SWIGLU_LIMIT, SWIGLU_LIMIT)
        act0 = (g0 * jax.nn.sigmoid(SWIGLU_ALPHA * g0) * (u0 + 1.0)).astype(BF16)
        for nt in range(acc_ref.shape[1] // tile):
            ns = slice(nt * tile, (nt + 1) * tile)
            wdt = wd_ref[0, :, ns].astype(BF16)
            wd_s[:, ns] = wdt
            acc_ref[0:MOE_SUB, ns] += jnp.dot(act0, wdt, preferred_element_type=F32)

        rest = nsub - 1

        def pair(s, carry):
            mlp_rows(pl.multiple_of(MOE_SUB + s * (2 * MOE_SUB), MOE_SUB), 2 * MOE_SUB)
            return carry
        lax.fori_loop(0, rest // 2, pair, 0)

        @pl.when(rest % 2 == 1)
        def _():
            mlp_rows(pl.multiple_of(nsub * MOE_SUB - MOE_SUB, MOE_SUB), MOE_SUB)

        @pl.when(f == nf - 1)
        def _():
            def fin(s, carry):
                r0 = pl.multiple_of(s * MOE_SUB, MOE_SUB)
                y = acc_ref[pl.ds(r0, MOE_SUB), :] + bd_ref[0]
                o_ref[pl.ds(r0, MOE_SUB), :] = y.astype(o_ref.dtype)
                return carry
            lax.fori_loop(0, nsub, fin, 0)
            lax.fori_loop(nsub, n_sub_max, zero_rows, 0)


def _moe_experts(xs, item_blk, item_exp, item_nsub, w_gate, b_gate, w_up, b_up,
                 w_down, b_down):
    P, D = xs.shape
    E, _, F = w_gate.shape
    n_items = item_blk.shape[0]
    tf = _tile(F, MOE_FCHUNK)
    nf = F // tf

    def fidx(f, ns):
        return jnp.where(ns > 0, f, nf - 1)

    grid_spec = pltpu.PrefetchScalarGridSpec(
        num_scalar_prefetch=3,
        grid=(n_items, nf),
        in_specs=[
            pl.BlockSpec((MOE_TILE, D), lambda w, f, blk, ex, ns: (blk[w], 0)),
            pl.BlockSpec((1, D, tf), lambda w, f, blk, ex, ns: (ex[w], 0, fidx(f, ns[w]))),
            pl.BlockSpec((1, 1, tf), lambda w, f, blk, ex, ns: (ex[w], 0, fidx(f, ns[w]))),
            pl.BlockSpec((1, D, tf), lambda w, f, blk, ex, ns: (ex[w], 0, fidx(f, ns[w]))),
            pl.BlockSpec((1, 1, tf), lambda w, f, blk, ex, ns: (ex[w], 0, fidx(f, ns[w]))),
            pl.BlockSpec((1, tf, D), lambda w, f, blk, ex, ns: (ex[w], fidx(f, ns[w]), 0)),
            pl.BlockSpec((1, 1, D), lambda w, f, blk, ex, ns: (ex[w], 0, 0)),
        ],
        out_specs=pl.BlockSpec((MOE_TILE, D), lambda w, f, blk, ex, ns: (w, 0)),
        scratch_shapes=[pltpu.VMEM((D, tf), BF16), pltpu.VMEM((D, tf), BF16),
                        pltpu.VMEM((tf, D), BF16), pltpu.VMEM((MOE_TILE, D), F32)],
    )
    return pl.pallas_call(
        _moe_kernel,
        grid_spec=grid_spec,
        out_shape=jax.ShapeDtypeStruct((P, D), BF16),
        compiler_params=_params(("arbitrary", "arbitrary")),
        name="moe_experts",
    )(item_blk, item_exp, item_nsub, xs, w_gate, b_gate.reshape(E, 1, F), w_up,
      b_up.reshape(E, 1, F), w_down, b_down.reshape(E, 1, D))


def _route(top_idx):
    N = top_idx.shape[0]
    A = N * TOP_K
    n_items = A // MOE_TILE + N_EXPERTS
    P = n_items * MOE_TILE
    flat_e = top_idx.reshape(A)
    order = jnp.argsort(flat_e).astype(I32)
    pos = jnp.argsort(order).astype(I32)
    eids = jnp.arange(N_EXPERTS, dtype=I32)
    counts = jnp.sum((flat_e[:, None] == eids[None, :]).astype(I32), axis=0)
    grp_start = jnp.cumsum(counts) - counts
    n_it = (counts + MOE_TILE - 1) // MOE_TILE
    it_end = jnp.cumsum(n_it)
    it_start = it_end - n_it
    per_item = (counts + jnp.maximum(n_it, 1) - 1) // jnp.maximum(n_it, 1)
    per_item = jnp.maximum((per_item + MOE_SUB - 1) // MOE_SUB * MOE_SUB, MOE_SUB)
    n_used = it_end[-1]
    items = jnp.arange(n_items, dtype=I32)
    last = jnp.maximum(n_used - 1, 0)
    item_blk = jnp.minimum(items, last)
    item_exp = jnp.minimum(jnp.sum((item_blk[:, None] >= it_end[None, :]).astype(I32), axis=1),
                           N_EXPERTS - 1)
    k_in = item_blk - it_start[item_exp]
    first = grp_start[item_exp] + k_in * per_item[item_exp]
    nvalid = jnp.clip(counts[item_exp] - k_in * per_item[item_exp], 0, per_item[item_exp])
    nvalid = jnp.where(items < n_used, nvalid, 0)
    item_nsub = ((nvalid + MOE_SUB - 1) // MOE_SUB).astype(I32)
    t = jnp.arange(MOE_TILE, dtype=I32)
    valid = t[None, :] < nvalid[:, None]
    src = order[jnp.clip(first[:, None] + t[None, :], 0, A - 1)]
    rows = jnp.arange(P, dtype=I32).reshape(n_items, MOE_TILE)
    row_tok = jnp.where(valid, src // TOP_K, rows % N).reshape(P).astype(I32)
    off = pos - grp_start[flat_e]
    kk = off // per_item[flat_e]
    dest = (it_start[flat_e] + kk) * MOE_TILE + off - kk * per_item[flat_e]
    return row_tok, dest.astype(I32), item_blk, item_exp, item_nsub


def _final_kernel(x1_ref, y_ref, p_ref, g2_ref, o_ref):
    p = p_ref[...]
    y = y_ref[0].astype(F32) * p[:, 0:1]
    for k in range(1, TOP_K):
        y = y + y_ref[k].astype(F32) * p[:, k:k + 1]
    o_ref[...] = x1_ref[...] + g2_ref[0] * y


def _final(x1, yk, probs, g2, B, S):
    N, D = x1.shape
    tm = _tile(S, 512)
    nS = S // tm
    return pl.pallas_call(
        _final_kernel,
        grid=(N // tm,),
        in_specs=[pl.BlockSpec((tm, D), lambda i: (i, 0)),
                  pl.BlockSpec((TOP_K, tm, D), lambda i: (0, i, 0)),
                  pl.BlockSpec((tm, LANES), lambda i: (i, 0)),
                  pl.BlockSpec((1, 1, D), lambda i: (i // nS, 0, 0))],
        out_specs=pl.BlockSpec((tm, D), lambda i: (i, 0)),
        out_shape=jax.ShapeDtypeStruct((N, D), F32),
        compiler_params=_params(("arbitrary",)),
        name="moe_combine",
    )(x1, yk, probs, g2.reshape(B, 1, D))


def _layer(x, c, w_ada, b_ada, norm1_g, w_in, conv_w, conv_b, conv_ln_g, conv_ln_b, w_conv_out,
           q_norm_g, k_norm_g, w_attn_out, w_out, norm2_g, w_router, b_router, w_gate, b_gate,
           w_up, b_up, w_down, b_down):
    B, S, D = x.shape
    N = B * S
    C = conv_w.shape[1]
    kvw = N_KV_HEADS * HEAD_DIM

    mod = _ada(c, w_ada, b_ada)
    sh1, sc1, g1, sh2, sc2, g2 = [mod[:, j * D:(j + 1) * D] for j in range(6)]

    h = _normmod(x, norm1_g, sc1, sh1).reshape(N, D)

    o = 0
    w_conv_in = w_in[:, o:o + 2 * C]; o += 2 * C
    w_q = w_in[:, o:o + ATTN_WIDTH]; o += ATTN_WIDTH
    w_kv_small = w_in[:, o:o + 2 * kvw + IDX_WIDTH + IDX_DIM + IDX_HEADS]
    w_k_v = w_in[:, o:o + 2 * kvw]; o += 2 * kvw
    w_iq = w_in[:, o:o + IDX_WIDTH]; o += IDX_WIDTH
    w_ik_iw = w_in[:, o:o + IDX_DIM + IDX_HEADS]; o += IDX_DIM + IDX_HEADS
    w_gates = w_in[:, o:o + 2 * D]
    del w_kv_small

    tm = _tile(S, 1024)
    nS = S // tm
    cos128, sin128 = _rope_tables(S, HEAD_DIM)
    cos64, sin64 = _rope_tables(S, IDX_DIM)
    tab = pl.BlockSpec((tm, LANES), lambda i, j: (i % nS, 0))
    vec = pl.BlockSpec((1, LANES), lambda i, j: (0, 0))

    tn = _tile(2 * C, 1024)
    hc = tn // 2
    wv = w_conv_in[:, :C].reshape(D, C // hc, hc)
    wg = w_conv_in[:, C:].reshape(D, C // hc, hc)
    w_glu = jnp.concatenate([wv, wg], axis=2).reshape(D, 2 * C).astype(BF16)
    u = _proj(_glu_kernel, h, w_glu, tn, [], [],
              jax.ShapeDtypeStruct((N, C), BF16),
              pl.BlockSpec((tm, hc), lambda i, j: (i, j)), tm, "proj_conv_glu")

    tq = _tile(ATTN_WIDTH, 512)
    q = _proj(_q_kernel, h, w_q.astype(BF16), tq, [cos128, sin128, q_norm_g.reshape(1, HEAD_DIM)],
              [tab, tab, vec], jax.ShapeDtypeStruct((N, ATTN_WIDTH), BF16),
              pl.BlockSpec((tm, tq), lambda i, j: (i, j)), tm, "proj_q")

    pad = LANES - IDX_DIM - IDX_HEADS
    w_kvs = jnp.concatenate([w_k_v, w_ik_iw, jnp.zeros((D, pad), F32)], axis=1).astype(BF16)
    nk = w_kvs.shape[1]
    blk = lambda w: pl.BlockSpec((tm, w), lambda i, j: (i, 0))
    cpt = tm // KEY_CHUNK
    k, vt, ika, iw = _proj(
        _kv_kernel, h, w_kvs, nk,
        [cos128, sin128, cos64, sin64, k_norm_g.reshape(1, HEAD_DIM)], [tab, tab, tab, tab, vec],
        [jax.ShapeDtypeStruct((N, kvw), BF16), jax.ShapeDtypeStruct((N // KEY_CHUNK, kvw, KEY_CHUNK), BF16),
         jax.ShapeDtypeStruct((N, LANES), BF16), jax.ShapeDtypeStruct((N, LANES), F32)],
        [blk(kvw), pl.BlockSpec((cpt, kvw, KEY_CHUNK), lambda i, j: (i, 0, 0)), blk(LANES), blk(LANES)],
        tm, "proj_kv_indexer_key")

    ti = _tile(IDX_WIDTH, 512)
    iq = _proj(_iq_kernel, h, w_iq.astype(BF16), ti, [cos64, sin64], [tab, tab],
               jax.ShapeDtypeStruct((N, IDX_WIDTH), BF16),
               pl.BlockSpec((tm, ti), lambda i, j: (i, j)), tm, "proj_iq")

    tg = _tile(2 * D, 1024)
    gates = _proj(_sigmoid_kernel, h, w_gates.astype(BF16), tg, [], [],
                  jax.ShapeDtypeStruct((N, 2 * D), BF16),
                  pl.BlockSpec((tm, tg), lambda i, j: (i, j)), tm, "proj_gates")

    yc = _conv_branch(u.reshape(B, S, C), conv_w, conv_b, conv_ln_g, conv_ln_b, w_conv_out,
                      gates.reshape(B, S, 2 * D)).reshape(N, D)
    ao = _sparse_attention(iq, iw, q, ika, k, vt, B, S)

    x1, h2, top_idx, probs = _mix(yc, ao, gates, x.reshape(N, D), g1, w_attn_out, w_out, norm2_g,
                                  sc2, sh2, w_router, b_router, B, S)

    row_tok, dest, item_blk, item_exp, item_nsub = _route(top_idx[:, :TOP_K])
    xs = h2.at[row_tok].get(mode="promise_in_bounds")
    ys = _moe_experts(xs, item_blk, item_exp, item_nsub, w_gate, b_gate, w_up, b_up, w_down,
                      b_down)
    yk = ys.at[dest.reshape(N, TOP_K).T].get(mode="promise_in_bounds")
    out = _final(x1, yk, probs, g2, B, S)
    return out.reshape(B, S, D)


def kernel(x, c, w_ada, b_ada, norm1_g, w_in, conv_w, conv_b, conv_ln_g, conv_ln_b, w_conv_out,
           q_norm_g, k_norm_g, w_attn_out, w_out, norm2_g, w_router, b_router, w_gate, b_gate,
           w_up, b_up, w_down, b_down):
    for l in range(w_ada.shape[0]):
        x = _layer(x, c, w_ada[l], b_ada[l], norm1_g[l], w_in[l], conv_w[l], conv_b[l],
                   conv_ln_g[l], conv_ln_b[l], w_conv_out[l], q_norm_g[l], k_norm_g[l],
                   w_attn_out[l], w_out[l], norm2_g[l], w_router[l], b_router[l], w_gate[l],
                   b_gate[l], w_up[l], b_up[l], w_down[l], b_down[l])
    return x
```

```python
import functools

import jax
import jax.numpy as jnp
from jax import lax
from jax.experimental import pallas as pl
from jax.experimental.pallas import tpu as pltpu

F32 = jnp.float32
BF16 = jnp.bfloat16
I32 = jnp.int32

EPS = 1e-6
CONV_TAPS = 31
N_HEADS = 8
N_KV_HEADS = 2
HEAD_DIM = 128
GROUP = N_HEADS // N_KV_HEADS
ATTN_WIDTH = N_HEADS * HEAD_DIM
ROPE_THETA = 10000.0
Q_BLOCK = 128
IDX_HEADS = 16
IDX_DIM = 64
IDX_WIDTH = IDX_HEADS * IDX_DIM
IDX_TOPK_MAX = 256
N_EXPERTS = 32
TOP_K = 4
SWIGLU_ALPHA = 1.702
SWIGLU_LIMIT = 7.0

LANES = 128
CONV_HALO = 32
PROJ_SLAB = 256
KEY_CHUNK = 256
MOE_TILE = 1024
MOE_SUB = 256
MOE_FCHUNK = 512
NEG_BIG = -1e30
INT_MIN = -2 ** 31
VMEM_LIMIT = 60 * 1024 * 1024


def _tile(dim, pref):
    if dim <= pref:
        return dim
    t = pref - pref % LANES
    while t >= LANES:
        if dim % t == 0:
            return t
        t -= LANES
    return dim


def _params(sem):
    return pltpu.CompilerParams(dimension_semantics=sem, vmem_limit_bytes=VMEM_LIMIT)


def _ada_kernel(c_ref, w_ref, b_ref, o_ref):
    c = c_ref[...]
    sc = c * jax.nn.sigmoid(c)
    o_ref[...] = jnp.dot(sc.astype(BF16), w_ref[...].astype(BF16),
                         preferred_element_type=F32) + b_ref[...]


def _ada(c, w_ada, b_ada):
    B, D = c.shape
    n = w_ada.shape[1]
    rows = 8
    cp = jnp.zeros((rows, D), F32).at[:B].set(c)
    tn = _tile(n, 1024)
    out = pl.pallas_call(
        _ada_kernel,
        grid=(n // tn,),
        in_specs=[pl.BlockSpec((rows, D), lambda j: (0, 0)),
                  pl.BlockSpec((D, tn), lambda j: (0, j)),
                  pl.BlockSpec((1, tn), lambda j: (0, j))],
        out_specs=pl.BlockSpec((rows, tn), lambda j: (0, j)),
        out_shape=jax.ShapeDtypeStruct((rows, n), F32),
        compiler_params=_params(("arbitrary",)),
        name="ada_mod",
    )(cp, w_ada, b_ada.reshape(1, n))
    return out[:B]


def _normmod_kernel(x_ref, g_ref, sc_ref, sh_ref, o_ref):
    x = x_ref[0]
    ms = jnp.mean(x * x, axis=-1, keepdims=True)
    y = x * lax.rsqrt(ms + EPS) * g_ref[...]
    o_ref[0] = (y * (1.0 + sc_ref[0]) + sh_ref[0]).astype(o_ref.dtype)


def _normmod(x, g, sc, sh):
    B, S, D = x.shape
    ts = _tile(S, 512)
    return pl.pallas_call(
        _normmod_kernel,
        grid=(B, S // ts),
        in_specs=[pl.BlockSpec((1, ts, D), lambda b, i: (b, i, 0)),
                  pl.BlockSpec((1, D), lambda b, i: (0, 0)),
                  pl.BlockSpec((1, 1, D), lambda b, i: (b, 0, 0)),
                  pl.BlockSpec((1, 1, D), lambda b, i: (b, 0, 0))],
        out_specs=pl.BlockSpec((1, ts, D), lambda b, i: (b, i, 0)),
        out_shape=jax.ShapeDtypeStruct((B, S, D), BF16),
        compiler_params=_params(("arbitrary", "arbitrary")),
        name="norm1_mod",
    )(x, g.reshape(1, D), sc.reshape(B, 1, D), sh.reshape(B, 1, D))


def _rope_tables(S, d):
    inv = ROPE_THETA ** (-jnp.arange(0, d, 2, dtype=F32) / d)
    ang = jnp.arange(S, dtype=F32)[:, None] * inv[None, :]
    cos = jnp.cos(ang)
    sin = jnp.sin(ang)
    reps = LANES // d
    cos_t = jnp.tile(jnp.concatenate([cos, cos], axis=1), (1, reps))
    sin_t = jnp.tile(jnp.concatenate([-sin, sin], axis=1), (1, reps))
    return cos_t, sin_t


def _rope128(y, cos, sin):
    return y * cos + pltpu.roll(y, 64, axis=1) * sin


def _rope64(y, cos, sin):
    lane = lax.broadcasted_iota(I32, y.shape, 1)
    rot = jnp.where((lane % 64) < 32, pltpu.roll(y, 96, axis=1), pltpu.roll(y, 32, axis=1))
    return y * cos + rot * sin


def _glu_kernel(x_ref, w_ref, o_ref):
    acc = jnp.dot(x_ref[...], w_ref[...], preferred_element_type=F32)
    half = acc.shape[1] // 2
    o_ref[...] = (acc[:, :half] * jax.nn.sigmoid(acc[:, half:])).astype(o_ref.dtype)


def _row_slabs(rows):
    step = min(rows, PROJ_SLAB)
    return [(r, step) for r in range(0, rows, step)]


def _q_kernel(x_ref, w_ref, cos_ref, sin_ref, g_ref, o_ref):
    g = g_ref[...]
    for r0, n in _row_slabs(x_ref.shape[0]):
        acc = jnp.dot(x_ref[r0:r0 + n, :], w_ref[...], preferred_element_type=F32)
        cos = cos_ref[r0:r0 + n, :]
        sin = sin_ref[r0:r0 + n, :]
        for h in range(acc.shape[1] // HEAD_DIM):
            xh = acc[:, h * HEAD_DIM:(h + 1) * HEAD_DIM]
            ms = jnp.mean(xh * xh, axis=-1, keepdims=True)
            y = xh * lax.rsqrt(ms + EPS) * g
            o_ref[r0:r0 + n, h * HEAD_DIM:(h + 1) * HEAD_DIM] = (
                _rope128(y, cos, sin) * (HEAD_DIM ** -0.5)).astype(o_ref.dtype)


def _kv_kernel(x_ref, w_ref, cos_ref, sin_ref, cos64_ref, sin64_ref, g_ref,
               k_ref, vt_ref, ika_ref, iw_ref):
    kvw = N_KV_HEADS * HEAD_DIM
    g = g_ref[...]
    for c in range(vt_ref.shape[0]):
        r0, n = c * KEY_CHUNK, KEY_CHUNK
        acc = jnp.dot(x_ref[r0:r0 + n, :], w_ref[...], preferred_element_type=F32)
        cos = cos_ref[r0:r0 + n, :]
        sin = sin_ref[r0:r0 + n, :]
        for h in range(N_KV_HEADS):
            xh = acc[:, h * HEAD_DIM:(h + 1) * HEAD_DIM]
            ms = jnp.mean(xh * xh, axis=-1, keepdims=True)
            y = xh * lax.rsqrt(ms + EPS) * g
            k_ref[r0:r0 + n, h * HEAD_DIM:(h + 1) * HEAD_DIM] = _rope128(y, cos, sin).astype(k_ref.dtype)
        vt_ref[c] = acc[:, kvw:2 * kvw].T.astype(vt_ref.dtype)
        tail = acc[:, 2 * kvw:2 * kvw + LANES]
        lane = lax.broadcasted_iota(I32, tail.shape, 1)
        roped = _rope64(tail, cos64_ref[r0:r0 + n, :], sin64_ref[r0:r0 + n, :])
        ika_ref[r0:r0 + n, :] = jnp.where(lane < IDX_DIM, roped, 0.0).astype(ika_ref.dtype)
        iw = pltpu.roll(tail, LANES - IDX_DIM, axis=1)
        iw_ref[r0:r0 + n, :] = jnp.where(lane < IDX_HEADS, iw * (IDX_HEADS ** -0.5), 0.0)


def _iq_kernel(x_ref, w_ref, cos_ref, sin_ref, o_ref):
    for r0, n in _row_slabs(x_ref.shape[0]):
        acc = jnp.dot(x_ref[r0:r0 + n, :], w_ref[...], preferred_element_type=F32)
        cos = cos_ref[r0:r0 + n, :]
        sin = sin_ref[r0:r0 + n, :]
        for p in range(acc.shape[1] // LANES):
            y = acc[:, p * LANES:(p + 1) * LANES]
            o_ref[r0:r0 + n, p * LANES:(p + 1) * LANES] = (
                _rope64(y, cos, sin) * (IDX_DIM ** -0.5)).astype(o_ref.dtype)


def _sigmoid_kernel(x_ref, w_ref, o_ref):
    acc = jnp.dot(x_ref[...], w_ref[...], preferred_element_type=F32)
    o_ref[...] = jax.nn.sigmoid(acc).astype(o_ref.dtype)


def _proj(kernel, h, w, tn, aux, aux_specs, out_shapes, out_specs, tm, name):
    N, D = h.shape
    n = w.shape[1]
    return pl.pallas_call(
        kernel,
        grid=(N // tm, n // tn),
        in_specs=[pl.BlockSpec((tm, D), lambda i, j: (i, 0)),
                  pl.BlockSpec((D, tn), lambda i, j: (0, j))] + aux_specs,
        out_specs=out_specs,
        out_shape=out_shapes,
        compiler_params=_params(("arbitrary", "arbitrary")),
        name=name,
    )(h, w, *aux)


def _conv_kernel(u_ref, halo_ref, cw_ref, cb_ref, lg_ref, lb_ref, wo_ref, gate_ref, o_ref,
                 buf_ref, sh_ref, conv_ref):
    i = pl.program_id(1)
    ts = u_ref.shape[1]
    SUBL = 8
    halo = halo_ref[0].astype(F32)
    buf_ref[0:CONV_HALO, :] = jnp.where(i == 0, 0.0, halo)
    buf_ref[CONV_HALO:, :] = u_ref[0].astype(F32)
    for b in range(1, SUBL):
        sh_ref[b - 1] = buf_ref[b:b + sh_ref.shape[1], :]
    rc = 32
    first = CONV_HALO - (CONV_TAPS - 1)

    def chunk(c, carry):
        r0 = pl.multiple_of(c * rc, rc)
        acc = jnp.zeros((rc, buf_ref.shape[1]), F32)
        for t in range(CONV_TAPS):
            a, b = divmod(first + t, SUBL)
            if b == 0:
                rows = buf_ref[pl.ds(r0 + a * SUBL, rc), :]
            else:
                rows = sh_ref[b - 1, pl.ds(r0 + a * SUBL, rc), :]
            acc = acc + cw_ref[t:t + 1, :] * rows
        conv_ref[pl.ds(r0, rc), :] = acc
        return carry

    lax.fori_loop(0, ts // rc, chunk, 0)
    u = conv_ref[...] + cb_ref[...]
    mu = jnp.mean(u, axis=-1, keepdims=True)
    uc = u - mu
    var = jnp.mean(uc * uc, axis=-1, keepdims=True)
    y = uc * lax.rsqrt(var + EPS) * lg_ref[...] + lb_ref[...]
    y = y * jax.nn.sigmoid(y)
    z = jnp.dot(y.astype(BF16), wo_ref[...], preferred_element_type=F32)
    o_ref[0] = (z * gate_ref[0].astype(F32)).astype(o_ref.dtype)


def _conv_branch(u, conv_w, conv_b, ln_g, ln_b, w_conv_out, gates):
    B, S, C = u.shape
    D = w_conv_out.shape[1]
    ts = _tile(S, 256)
    hb = ts // CONV_HALO
    cw = jnp.zeros((CONV_HALO, C), F32).at[:CONV_TAPS].set(conv_w)
    return pl.pallas_call(
        _conv_kernel,
        grid=(B, S // ts),
        in_specs=[pl.BlockSpec((1, ts, C), lambda b, i: (b, i, 0)),
                  pl.BlockSpec((1, CONV_HALO, C), lambda b, i: (b, jnp.maximum(i * hb - 1, 0), 0)),
                  pl.BlockSpec((CONV_HALO, C), lambda b, i: (0, 0)),
                  pl.BlockSpec((1, C), lambda b, i: (0, 0)),
                  pl.BlockSpec((1, C), lambda b, i: (0, 0)),
                  pl.BlockSpec((1, C), lambda b, i: (0, 0)),
                  pl.BlockSpec((C, D), lambda b, i: (0, 0)),
                  pl.BlockSpec((1, ts, D), lambda b, i: (b, i, 0))],
        out_specs=pl.BlockSpec((1, ts, D), lambda b, i: (b, i, 0)),
        out_shape=jax.ShapeDtypeStruct((B, S, D), BF16),
        scratch_shapes=[pltpu.VMEM((ts + CONV_HALO, C), F32),
                        pltpu.VMEM((7, ts + CONV_HALO - 8, C), F32),
                        pltpu.VMEM((ts, C), F32)],
        compiler_params=_params(("arbitrary", "arbitrary")),
        name="conv_branch",
    )(u, u, cw, conv_b.reshape(1, C), ln_g.reshape(1, C), ln_b.reshape(1, C),
      w_conv_out.astype(BF16), gates)


def _attn_kernel(n_sel, iq_ref, iw_ref, q_ref, ika_ref, k_ref, vt_ref, o_ref,
                 key_ref, iqt_ref, wt_ref, qt_ref, bound_ref, s_ref, acc_ref):
    i = pl.program_id(1)
    QB = Q_BLOCK
    CK = KEY_CHUNK
    GW = GROUP * QB
    SUBL = 8
    nck = (i * QB + QB + CK - 1) // CK
    q_pos = i * QB + lax.broadcasted_iota(I32, (CK, QB), 1)
    krow = lax.broadcasted_iota(I32, (CK, QB), 0)
    row8 = krow[0:SUBL]

    iq = iq_ref[...].astype(F32)
    iqt_ref[IDX_DIM:, :] = jnp.zeros((LANES - IDX_DIM, IDX_HEADS * QB), BF16)
    for p in range(IDX_HEADS // 2):
        t = iq[:, p * LANES:(p + 1) * LANES].T
        iqt_ref[0:IDX_DIM, (2 * p) * QB:(2 * p + 1) * QB] = t[0:IDX_DIM].astype(BF16)
        iqt_ref[0:IDX_DIM, (2 * p + 1) * QB:(2 * p + 2) * QB] = t[IDX_DIM:].astype(BF16)
    wt_ref[...] = iw_ref[...].T
    qf = q_ref[...].astype(F32)
    for h in range(N_HEADS):
        qt_ref[:, h * QB:(h + 1) * QB] = qf[:, h * HEAD_DIM:(h + 1) * HEAD_DIM].T.astype(BF16)

    def score_chunk(c, carry):
        k0 = pl.multiple_of(c * CK, CK)
        ka = ika_ref[pl.ds(k0, CK), :]
        acc = jnp.zeros((CK, QB), F32)
        for hg in range(IDX_HEADS // 4):
            st = jnp.dot(ka, iqt_ref[:, hg * 4 * QB:(hg + 1) * 4 * QB], preferred_element_type=F32)
            for j in range(4):
                h = hg * 4 + j
                acc = acc + jnp.maximum(st[:, j * QB:(j + 1) * QB], 0.0) * wt_ref[h:h + 1, :]
        key_ref[c] = jnp.where(k0 + krow <= q_pos, acc, -jnp.inf)
        return carry

    lax.fori_loop(0, nck, score_chunk, 0)

    def count(pred):
        def body(c, cnts):
            cnts = list(cnts)
            for r in range(CK // SUBL):
                sc = key_ref[c, r * SUBL:(r + 1) * SUBL, :]
                kidx = c * CK + r * SUBL + row8
                cnts[r % 4] = cnts[r % 4] + pred(sc, kidx).astype(I32)
            return tuple(cnts)
        z = jnp.zeros((SUBL, QB), I32)
        cnts = lax.fori_loop(0, nck, body, (z, z, z, z))
        tot = (cnts[0] + cnts[1]) + (cnts[2] + cnts[3])
        return jnp.broadcast_to(jnp.sum(tot, axis=0, keepdims=True), (SUBL, QB))

    def code_value(code):
        return pltpu.bitcast(code ^ ((code >> 31) & 0x7FFFFFFF), F32)

    zero = jnp.zeros((SUBL, QB), I32)
    zero_f = jnp.zeros((SUBL, QB), F32)
    code = jnp.where(count(lambda sc, kidx: sc >= zero_f) >= n_sel, zero, INT_MIN)

    def bit_step(b, code):
        cand = code | (1 << (30 - b))
        cand_f = code_value(cand)
        return jnp.where(count(lambda sc, kidx: sc >= cand_f) >= n_sel, cand, code)

    code = lax.fori_loop(0, 31, bit_step, code)
    thr = jnp.where(code == INT_MIN, -jnp.inf, code_value(code))

    n_gt = count(lambda sc, kidx: sc > thr)
    n_ge = count(lambda sc, kidx: sc >= thr)
    need = n_sel - n_gt
    bound_ref[...] = jnp.full((SUBL, QB), 2 ** 30, I32)

    @pl.when(jnp.max(jnp.where(code > INT_MIN, n_ge, 0)) > n_sel)
    def _():
        def idx_step(b, bound):
            cand = bound | (1 << (14 - b))
            n_tie = count(lambda sc, kidx: (sc == thr) & (kidx < cand))
            return jnp.where(n_tie <= need, cand, bound)
        bound_ref[...] = lax.fori_loop(0, 15, idx_step, zero)

    bound1 = bound_ref[0:1, :]
    thr1 = thr[0:1, :]

    def logit_chunk(c, mx):
        k0 = pl.multiple_of(c * CK, CK)
        sc = key_ref[c]
        kidx = k0 + krow
        sel = ((sc > thr1) | ((sc == thr1) & (kidx < bound1))) & (kidx <= q_pos)
        bias1 = jnp.where(sel, 0.0, NEG_BIG)
        bias = jnp.concatenate([bias1] * GROUP, axis=1)
        out = []
        for kh in range(N_KV_HEADS):
            kc = k_ref[pl.ds(k0, CK), kh * HEAD_DIM:(kh + 1) * HEAD_DIM]
            st = jnp.dot(kc, qt_ref[:, kh * GW:(kh + 1) * GW], preferred_element_type=F32) + bias
            s_ref[c, :, kh * GW:(kh + 1) * GW] = st
            m8 = mx[kh]
            for r in range(CK // SUBL):
                m8 = jnp.maximum(m8, st[r * SUBL:(r + 1) * SUBL, :])
            out.append(m8)
        return tuple(out)

    mx = lax.fori_loop(0, nck, logit_chunk,
                       tuple(jnp.full((SUBL, GW), NEG_BIG, F32) for _ in range(N_KV_HEADS)))
    m_fin = [jnp.max(m8, axis=0, keepdims=True) for m8 in mx]

    acc_ref[...] = jnp.zeros(acc_ref.shape, F32)

    def value_chunk(c, ls):
        out = []
        for kh in range(N_KV_HEADS):
            p = jnp.exp(s_ref[c, :, kh * GW:(kh + 1) * GW] - m_fin[kh])
            l8 = ls[kh]
            for r in range(CK // SUBL):
                l8 = l8 + p[r * SUBL:(r + 1) * SUBL, :]
            out.append(l8)
            vt = vt_ref[c, kh * HEAD_DIM:(kh + 1) * HEAD_DIM, :]
            acc_ref[kh] += jnp.dot(vt, p.astype(BF16), preferred_element_type=F32)
        return tuple(out)

    ls = lax.fori_loop(0, nck, value_chunk,
                       tuple(jnp.zeros((SUBL, GW), F32) for _ in range(N_KV_HEADS)))
    for kh in range(N_KV_HEADS):
        ot = acc_ref[kh] / jnp.sum(ls[kh], axis=0, keepdims=True)
        for g in range(GROUP):
            h = kh * GROUP + g
            o_ref[:, h * HEAD_DIM:(h + 1) * HEAD_DIM] = ot[:, g * QB:(g + 1) * QB].T.astype(o_ref.dtype)


def _sparse_attention(iq, iw, q, ika, k, vt, B, S):
    N = B * S
    nb = S // Q_BLOCK
    n_sel = min(IDX_TOPK_MAX, S // 4)
    kvw = N_KV_HEADS * HEAD_DIM
    nchunks = S // KEY_CHUNK
    gw = GROUP * Q_BLOCK
    qspec = lambda w: pl.BlockSpec((Q_BLOCK, w), lambda b, i: (b * nb + i, 0))
    kspec = lambda w: pl.BlockSpec((S, w), lambda b, i: (b, 0))
    return pl.pallas_call(
        functools.partial(_attn_kernel, n_sel),
        grid=(B, nb),
        in_specs=[qspec(IDX_WIDTH), qspec(LANES), qspec(ATTN_WIDTH), kspec(LANES), kspec(kvw),
                  pl.BlockSpec((nchunks, kvw, KEY_CHUNK), lambda b, i: (b, 0, 0))],
        out_specs=qspec(ATTN_WIDTH),
        out_shape=jax.ShapeDtypeStruct((N, ATTN_WIDTH), BF16),
        scratch_shapes=[pltpu.VMEM((nchunks, KEY_CHUNK, Q_BLOCK), F32),
                        pltpu.VMEM((LANES, IDX_HEADS * Q_BLOCK), BF16),
                        pltpu.VMEM((LANES, Q_BLOCK), F32),
                        pltpu.VMEM((HEAD_DIM, N_HEADS * Q_BLOCK), BF16),
                        pltpu.VMEM((8, Q_BLOCK), I32),
                        pltpu.VMEM((nchunks, KEY_CHUNK, N_KV_HEADS * gw), F32),
                        pltpu.VMEM((N_KV_HEADS, HEAD_DIM, gw), F32)],
        compiler_params=_params(("arbitrary", "arbitrary")),
        name="dsa_attention",
    )(iq, iw, q, ika, k, vt)


def _split_bf16(a):
    hi = a.astype(BF16)
    lo = (a - hi.astype(F32)).astype(BF16)
    return hi, lo


def _mix_kernel(yc_ref, ao_ref, gate_ref, x_ref, g1_ref, wao_ref, wo_ref, n2_ref, sc2_ref, sh2_ref,
                wrh_ref, wrl_ref, br_ref, x1_ref, h2_ref, idx_ref, prob_ref):
    for r0, n in [(0, x_ref.shape[0])]:
        rows = slice(r0, r0 + n)
        ya = jnp.dot(ao_ref[rows, :], wao_ref[...], preferred_element_type=F32)
        mixed = yc_ref[rows, :].astype(F32) + gate_ref[rows, :].astype(F32) * ya
        z = jnp.dot(mixed.astype(BF16), wo_ref[...], preferred_element_type=F32)
        x1 = x_ref[rows, :] + g1_ref[0] * z
        x1_ref[rows, :] = x1
        ms = jnp.mean(x1 * x1, axis=-1, keepdims=True)
        h2 = x1 * lax.rsqrt(ms + EPS) * n2_ref[...] * (1.0 + sc2_ref[0]) + sh2_ref[0]
        h2_ref[rows, :] = h2.astype(h2_ref.dtype)
        hh, hl = _split_bf16(h2)
        logits = (jnp.dot(hh, wrh_ref[...], preferred_element_type=F32)
                  + jnp.dot(hh, wrl_ref[...], preferred_element_type=F32)
                  + jnp.dot(hl, wrh_ref[...], preferred_element_type=F32)) + br_ref[...]
        lane = lax.broadcasted_iota(I32, logits.shape, 1)
        work = jnp.where(lane < N_EXPERTS, logits, -jnp.inf)
        vals = []
        idx_out = jnp.zeros(logits.shape, I32)
        for k in range(TOP_K):
            mv = jnp.max(work, axis=-1, keepdims=True)
            mi = jnp.min(jnp.where(work == mv, lane, LANES), axis=-1, keepdims=True)
            idx_out = jnp.where(lane == k, mi, idx_out)
            work = jnp.where(lane == mi, -jnp.inf, work)
            vals.append(mv)
        es = [jnp.exp(vk - vals[0]) for vk in vals]
        den = es[0]
        for e in es[1:]:
            den = den + e
        prob_out = jnp.zeros(logits.shape, F32)
        for k in range(TOP_K):
            prob_out = jnp.where(lane == k, es[k] / den, prob_out)
        idx_ref[rows, :] = idx_out
        prob_ref[rows, :] = prob_out


def _mix(yc, ao, gates, x, g1, w_attn_out, w_out, norm2_g, sc2, sh2, w_router, b_router, B, S):
    N, D = x.shape
    tm = _tile(S, 512)
    nS = S // tm
    wr = jnp.zeros((D, LANES), F32).at[:, :N_EXPERTS].set(w_router)
    wrh, wrl = _split_bf16(wr)
    br = jnp.zeros((1, LANES), F32).at[0, :N_EXPERTS].set(b_router)
    row = lambda w, c=0: pl.BlockSpec((tm, w), lambda i: (i, c))
    full = lambda a: pl.BlockSpec(a.shape, lambda i: (0,) * a.ndim)
    per_b = pl.BlockSpec((1, 1, D), lambda i: (i // nS, 0, 0))
    wao = w_attn_out.astype(BF16)
    wo = w_out.astype(BF16)
    n2 = norm2_g.reshape(1, D)
    return pl.pallas_call(
        _mix_kernel,
        grid=(N // tm,),
        in_specs=[row(D), row(ATTN_WIDTH), row(D, 1), row(D), per_b, full(wao), full(wo),
                  full(n2), per_b, per_b, full(wrh), full(wrl), full(br)],
        out_specs=[row(D), row(D), row(LANES), row(LANES)],
        out_shape=[jax.ShapeDtypeStruct((N, D), F32), jax.ShapeDtypeStruct((N, D), BF16),
                   jax.ShapeDtypeStruct((N, LANES), I32), jax.ShapeDtypeStruct((N, LANES), F32)],
        compiler_params=_params(("arbitrary",)),
        name="mix_router",
    )(yc, ao, gates, x, g1.reshape(B, 1, D), wao, wo, n2, sc2.reshape(B, 1, D),
      sh2.reshape(B, 1, D), wrh, wrl, br)


def _moe_kernel(blk_ref, exp_ref, nsub_ref, x_ref, wg_ref, bg_ref, wu_ref, bu_ref,
                wd_ref, bd_ref, o_ref, wg_s, wu_s, wd_s, acc_ref):
    w = pl.program_id(0)
    f = pl.program_id(1)
    nf = pl.num_programs(1)
    nsub = nsub_ref[w]
    n_sub_max = o_ref.shape[0] // MOE_SUB

    def zero_rows(s, carry):
        r0 = pl.multiple_of(s * MOE_SUB, MOE_SUB)
        o_ref[pl.ds(r0, MOE_SUB), :] = jnp.zeros((MOE_SUB, o_ref.shape[1]), o_ref.dtype)
        return carry

    @pl.when((nsub == 0) & (f == 0))
    def _():
        lax.fori_loop(0, n_sub_max, zero_rows, 0)

    @pl.when(nsub > 0)
    def _():
        @pl.when(f == 0)
        def _():
            def zero(s, carry):
                r0 = pl.multiple_of(s * MOE_SUB, MOE_SUB)
                acc_ref[pl.ds(r0, MOE_SUB), :] = jnp.zeros((MOE_SUB, acc_ref.shape[1]), F32)
                return carry
            lax.fori_loop(0, nsub, zero, 0)

        def mlp_rows(r0, rows):
            xs = x_ref[pl.ds(r0, rows), :]
            g = jnp.dot(xs, wg_s[...], preferred_element_type=F32) + bg_ref[0]
            u = jnp.dot(xs, wu_s[...], preferred_element_type=F32) + bu_ref[0]
            g = jnp.minimum(g, SWIGLU_LIMIT)
            u = jnp.clip(u, -SWIGLU_LIMIT, SWIGLU_LIMIT)
            act = g * jax.nn.sigmoid(SWIGLU_ALPHA * g) * (u + 1.0)
            acc_ref[pl.ds(r0, rows), :] += jnp.dot(act.astype(BF16), wd_s[...],
                                                   preferred_element_type=F32)

        xs0 = x_ref[0:MOE_SUB, :]
        tile = 2 * LANES
        g0 = jnp.zeros((MOE_SUB, wg_s.shape[1]), F32) + bg_ref[0]
        u0 = jnp.zeros((MOE_SUB, wu_s.shape[1]), F32) + bu_ref[0]
        for kt in range(xs0.shape[1] // tile):
            ks = slice(kt * tile, (kt + 1) * tile)
            wgt = wg_ref[0, ks, :].astype(BF16)
            wut = wu_ref[0, ks, :].astype(BF16)
            wg_s[ks, :] = wgt
            wu_s[ks, :] = wut
            g0 = g0 + jnp.dot(xs0[:, ks], wgt, preferred_element_type=F32)
            u0 = u0 + jnp.dot(xs0[:, ks], wut, preferred_element_type=F32)
        g0 = jnp.minimum(g0, SWIGLU_LIMIT)
        u0 = jnp.clip(u0, -SWIGLU_LIMIT, SWIGLU_LIMIT)
        act0 = (g0 * jax.nn.sigmoid(SWIGLU_ALPHA * g0) * (u0 + 1.0)).astype(BF16)
        for nt in range(acc_ref.shape[1] // tile):
            ns = slice(nt * tile, (nt + 1) * tile)
            wdt = wd_ref[0, :, ns].astype(BF16)
            wd_s[:, ns] = wdt
            acc_ref[0:MOE_SUB, ns] += jnp.dot(act0, wdt, preferred_element_type=F32)

        rest = nsub - 1

        def pair(s, carry):
            mlp_rows(pl.multiple_of(MOE_SUB + s * (2 * MOE_SUB), MOE_SUB), 2 * MOE_SUB)
            return carry
        lax.fori_loop(0, rest // 2, pair, 0)

        @pl.when(rest % 2 == 1)
        def _():
            mlp_rows(pl.multiple_of(nsub * MOE_SUB - MOE_SUB, MOE_SUB), MOE_SUB)

        @pl.when(f == nf - 1)
        def _():
            def fin(s, carry):
                r0 = pl.multiple_of(s * MOE_SUB, MOE_SUB)
                y = acc_ref[pl.ds(r0, MOE_SUB), :] + bd_ref[0]
                o_ref[pl.ds(r0, MOE_SUB), :] = y.astype(o_ref.dtype)
                return carry
            lax.fori_loop(0, nsub, fin, 0)
            lax.fori_loop(nsub, n_sub_max, zero_rows, 0)


def _moe_experts(xs, item_blk, item_exp, item_nsub, w_gate, b_gate, w_up, b_up,
                 w_down, b_down):
    P, D = xs.shape
    E, _, F = w_gate.shape
    n_items = item_blk.shape[0]
    tf = _tile(F, MOE_FCHUNK)
    nf = F // tf

    def fidx(f, ns):
        return jnp.where(ns > 0, f, nf - 1)

    grid_spec = pltpu.PrefetchScalarGridSpec(
        num_scalar_prefetch=3,
        grid=(n_items, nf),
        in_specs=[
            pl.BlockSpec((MOE_TILE, D), lambda w, f, blk, ex, ns: (blk[w], 0)),
            pl.BlockSpec((1, D, tf), lambda w, f, blk, ex, ns: (ex[w], 0, fidx(f, ns[w]))),
            pl.BlockSpec((1, 1, tf), lambda w, f, blk, ex, ns: (ex[w], 0, fidx(f, ns[w]))),
            pl.BlockSpec((1, D, tf), lambda w, f, blk, ex, ns: (ex[w], 0, fidx(f, ns[w]))),
            pl.BlockSpec((1, 1, tf), lambda w, f, blk, ex, ns: (ex[w], 0, fidx(f, ns[w]))),
            pl.BlockSpec((1, tf, D), lambda w, f, blk, ex, ns: (ex[w], fidx(f, ns[w]), 0)),
            pl.BlockSpec((1, 1, D), lambda w, f, blk, ex, ns: (ex[w], 0, 0)),
        ],
        out_specs=pl.BlockSpec((MOE_TILE, D), lambda w, f, blk, ex, ns: (w, 0)),
        scratch_shapes=[pltpu.VMEM((D, tf), BF16), pltpu.VMEM((D, tf), BF16),
                        pltpu.VMEM((tf, D), BF16), pltpu.VMEM((MOE_TILE, D), F32)],
    )
    return pl.pallas_call(
        _moe_kernel,
        grid_spec=grid_spec,
        out_shape=jax.ShapeDtypeStruct((P, D), BF16),
        compiler_params=_params(("arbitrary", "arbitrary")),
        name="moe_experts",
    )(item_blk, item_exp, item_nsub, xs, w_gate, b_gate.reshape(E, 1, F), w_up,
      b_up.reshape(E, 1, F), w_down, b_down.reshape(E, 1, D))


def _route(top_idx):
    N = top_idx.shape[0]
    A = N * TOP_K
    n_items = A // MOE_TILE + N_EXPERTS
    P = n_items * MOE_TILE
    flat_e = top_idx.reshape(A)
    order = jnp.argsort(flat_e).astype(I32)
    pos = jnp.argsort(order).astype(I32)
    eids = jnp.arange(N_EXPERTS, dtype=I32)
    counts = jnp.sum((flat_e[:, None] == eids[None, :]).astype(I32), axis=0)
    grp_start = jnp.cumsum(counts) - counts
    n_it = (counts + MOE_TILE - 1) // MOE_TILE
    it_end = jnp.cumsum(n_it)
    it_start = it_end - n_it
    per_item = (counts + jnp.maximum(n_it, 1) - 1) // jnp.maximum(n_it, 1)
    per_item = jnp.maximum((per_item + MOE_SUB - 1) // MOE_SUB * MOE_SUB, MOE_SUB)
    n_used = it_end[-1]
    items = jnp.arange(n_items, dtype=I32)
    last = jnp.maximum(n_used - 1, 0)
    item_blk = jnp.minimum(items, last)
    item_exp = jnp.minimum(jnp.sum((item_blk[:, None] >= it_end[None, :]).astype(I32), axis=1),
                           N_EXPERTS - 1)
    k_in = item_blk - it_start[item_exp]
    first = grp_start[item_exp] + k_in * per_item[item_exp]
    nvalid = jnp.clip(counts[item_exp] - k_in * per_item[item_exp], 0, per_item[item_exp])
    nvalid = jnp.where(items < n_used, nvalid, 0)
    item_nsub = ((nvalid + MOE_SUB - 1) // MOE_SUB).astype(I32)
    t = jnp.arange(MOE_TILE, dtype=I32)
    valid = t[None, :] < nvalid[:, None]
    src = order[jnp.clip(first[:, None] + t[None, :], 0, A - 1)]
    rows = jnp.arange(P, dtype=I32).reshape(n_items, MOE_TILE)
    row_tok = jnp.where(valid, src // TOP_K, rows % N).reshape(P).astype(I32)
    off = pos - grp_start[flat_e]
    kk = off // per_item[flat_e]
    dest = (it_start[flat_e] + kk) * MOE_TILE + off - kk * per_item[flat_e]
    return row_tok, dest.astype(I32), item_blk, item_exp, item_nsub


def _final_kernel(x1_ref, y_ref, p_ref, g2_ref, o_ref):
    p = p_ref[...]
    y = y_ref[0].astype(F32) * p[:, 0:1]
    for k in range(1, TOP_K):
        y = y + y_ref[k].astype(F32) * p[:, k:k + 1]
    o_ref[...] = x1_ref[...] + g2_ref[0] * y


def _final(x1, yk, probs, g2, B, S):
    N, D = x1.shape
    tm = _tile(S, 512)
    nS = S // tm
    return pl.pallas_call(
        _final_kernel,
        grid=(N // tm,),
        in_specs=[pl.BlockSpec((tm, D), lambda i: (i, 0)),
                  pl.BlockSpec((TOP_K, tm, D), lambda i: (0, i, 0)),
                  pl.BlockSpec((tm, LANES), lambda i: (i, 0)),
                  pl.BlockSpec((1, 1, D), lambda i: (i // nS, 0, 0))],
        out_specs=pl.BlockSpec((tm, D), lambda i: (i, 0)),
        out_shape=jax.ShapeDtypeStruct((N, D), F32),
        compiler_params=_params(("arbitrary",)),
        name="moe_combine",
    )(x1, yk, probs, g2.reshape(B, 1, D))


def _layer(x, c, w_ada, b_ada, norm1_g, w_in, conv_w, conv_b, conv_ln_g, conv_ln_b, w_conv_out,
           q_norm_g, k_norm_g, w_attn_out, w_out, norm2_g, w_router, b_router, w_gate, b_gate,
           w_up, b_up, w_down, b_down):
    B, S, D = x.shape
    N = B * S
    C = conv_w.shape[1]
    kvw = N_KV_HEADS * HEAD_DIM

    mod = _ada(c, w_ada, b_ada)
    sh1, sc1, g1, sh2, sc2, g2 = [mod[:, j * D:(j + 1) * D] for j in range(6)]

    h = _normmod(x, norm1_g, sc1, sh1).reshape(N, D)

    o = 0
    w_conv_in = w_in[:, o:o + 2 * C]; o += 2 * C
    w_q = w_in[:, o:o + ATTN_WIDTH]; o += ATTN_WIDTH
    w_kv_small = w_in[:, o:o + 2 * kvw + IDX_WIDTH + IDX_DIM + IDX_HEADS]
    w_k_v = w_in[:, o:o + 2 * kvw]; o += 2 * kvw
    w_iq = w_in[:, o:o + IDX_WIDTH]; o += IDX_WIDTH
    w_ik_iw = w_in[:, o:o + IDX_DIM + IDX_HEADS]; o += IDX_DIM + IDX_HEADS
    w_gates = w_in[:, o:o + 2 * D]
    del w_kv_small

    tm = _tile(S, 1024)
    nS = S // tm
    cos128, sin128 = _rope_tables(S, HEAD_DIM)
    cos64, sin64 = _rope_tables(S, IDX_DIM)
    tab = pl.BlockSpec((tm, LANES), lambda i, j: (i % nS, 0))
    vec = pl.BlockSpec((1, LANES), lambda i, j: (0, 0))

    tn = _tile(2 * C, 1024)
    hc = tn // 2
    wv = w_conv_in[:, :C].reshape(D, C // hc, hc)
    wg = w_conv_in[:, C:].reshape(D, C // hc, hc)
    w_glu = jnp.concatenate([wv, wg], axis=2).reshape(D, 2 * C).astype(BF16)
    u = _proj(_glu_kernel, h, w_glu, tn, [], [],
              jax.ShapeDtypeStruct((N, C), BF16),
              pl.BlockSpec((tm, hc), lambda i, j: (i, j)), tm, "proj_conv_glu")

    tq = _tile(ATTN_WIDTH, 512)
    q = _proj(_q_kernel, h, w_q.astype(BF16), tq, [cos128, sin128, q_norm_g.reshape(1, HEAD_DIM)],
              [tab, tab, vec], jax.ShapeDtypeStruct((N, ATTN_WIDTH), BF16),
              pl.BlockSpec((tm, tq), lambda i, j: (i, j)), tm, "proj_q")

    pad = LANES - IDX_DIM - IDX_HEADS
    w_kvs = jnp.concatenate([w_k_v, w_ik_iw, jnp.zeros((D, pad), F32)], axis=1).astype(BF16)
    nk = w_kvs.shape[1]
    blk = lambda w: pl.BlockSpec((tm, w), lambda i, j: (i, 0))
    cpt = tm // KEY_CHUNK
    k, vt, ika, iw = _proj(
        _kv_kernel, h, w_kvs, nk,
        [cos128, sin128, cos64, sin64, k_norm_g.reshape(1, HEAD_DIM)], [tab, tab, tab, tab, vec],
        [jax.ShapeDtypeStruct((N, kvw), BF16), jax.ShapeDtypeStruct((N // KEY_CHUNK, kvw, KEY_CHUNK), BF16),
         jax.ShapeDtypeStruct((N, LANES), BF16), jax.ShapeDtypeStruct((N, LANES), F32)],
        [blk(kvw), pl.BlockSpec((cpt, kvw, KEY_CHUNK), lambda i, j: (i, 0, 0)), blk(LANES), blk(LANES)],
        tm, "proj_kv_indexer_key")

    ti = _tile(IDX_WIDTH, 512)
    iq = _proj(_iq_kernel, h, w_iq.astype(BF16), ti, [cos64, sin64], [tab, tab],
               jax.ShapeDtypeStruct((N, IDX_WIDTH), BF16),
               pl.BlockSpec((tm, ti), lambda i, j: (i, j)), tm, "proj_iq")

    tg = _tile(2 * D, 1024)
    gates = _proj(_sigmoid_kernel, h, w_gates.astype(BF16), tg, [], [],
                  jax.ShapeDtypeStruct((N, 2 * D), BF16),
                  pl.BlockSpec((tm, tg), lambda i, j: (i, j)), tm, "proj_gates")

    yc = _conv_branch(u.reshape(B, S, C), conv_w, conv_b, conv_ln_g, conv_ln_b, w_conv_out,
                      gates.reshape(B, S, 2 * D)).reshape(N, D)
    ao = _sparse_attention(iq, iw, q, ika, k, vt, B, S)

    x1, h2, top_idx, probs = _mix(yc, ao, gates, x.reshape(N, D), g1, w_attn_out, w_out, norm2_g,
                                  sc2, sh2, w_router, b_router, B, S)

    row_tok, dest, item_blk, item_exp, item_nsub = _route(top_idx[:, :TOP_K])
    xs = h2.at[row_tok].get(mode="promise_in_bounds")
    ys = _moe_experts(xs, item_blk, item_exp, item_nsub, w_gate, b_gate, w_up, b_up, w_down,
                      b_down)
    yk = ys.at[dest.reshape(N, TOP_K).T].get(mode="promise_in_bounds")
    out = _final(x1, yk, probs, g2, B, S)
    return out.reshape(B, S, D)


def kernel(x, c, w_ada, b_ada, norm1_g, w_in, conv_w, conv_b, conv_ln_g, conv_ln_b, w_conv_out,
           q_norm_g, k_norm_g, w_attn_out, w_out, norm2_g, w_router, b_router, w_gate, b_gate,
           w_up, b_up, w_down, b_down):
    for l in range(w_ada.shape[0]):
        x = _layer(x, c, w_ada[l], b_ada[l], norm1_g[l], w_in[l], conv_w[l], conv_b[l],
                   conv_ln_g[l], conv_ln_b[l], w_conv_out[l], q_norm_g[l], k_norm_g[l],
                   w_attn_out[l], w_out[l], norm2_g[l], w_router[l], b_router[l], w_gate[l],
                   b_gate[l], w_up[l], b_up[l], w_down[l], b_down[l])
    return x
```

```python
import functools

import jax
import jax.numpy as jnp
from jax import lax
from jax.experimental import pallas as pl
from jax.experimental.pallas import tpu as pltpu

F32 = jnp.float32
BF16 = jnp.bfloat16
I32 = jnp.int32

EPS = 1e-6
CONV_TAPS = 31
N_HEADS = 8
N_KV_HEADS = 2
HEAD_DIM = 128
GROUP = N_HEADS // N_KV_HEADS
ATTN_WIDTH = N_HEADS * HEAD_DIM
ROPE_THETA = 10000.0
Q_BLOCK = 128
IDX_HEADS = 16
IDX_DIM = 64
IDX_WIDTH = IDX_HEADS * IDX_DIM
IDX_TOPK_MAX = 256
N_EXPERTS = 32
TOP_K = 4
SWIGLU_ALPHA = 1.702
SWIGLU_LIMIT = 7.0

LANES = 128
CONV_HALO = 32
PROJ_SLAB = 256
KEY_CHUNK = 512
MOE_TILE = 1024
MOE_SUB = 256
MOE_FCHUNK = 512
NEG_BIG = -1e30
INT_MIN = -2 ** 31
VMEM_LIMIT = 60 * 1024 * 1024


def _tile(dim, pref):
    if dim <= pref:
        return dim
    t = pref - pref % LANES
    while t >= LANES:
        if dim % t == 0:
            return t
        t -= LANES
    return dim


def _params(sem):
    return pltpu.CompilerParams(dimension_semantics=sem, vmem_limit_bytes=VMEM_LIMIT)


def _ada_kernel(c_ref, w_ref, b_ref, o_ref):
    c = c_ref[...]
    sc = c * jax.nn.sigmoid(c)
    o_ref[...] = jnp.dot(sc.astype(BF16), w_ref[...].astype(BF16),
                         preferred_element_type=F32) + b_ref[...]


def _ada(c, w_ada, b_ada):
    B, D = c.shape
    n = w_ada.shape[1]
    rows = 8
    cp = jnp.zeros((rows, D), F32).at[:B].set(c)
    tn = _tile(n, 1024)
    out = pl.pallas_call(
        _ada_kernel,
        grid=(n // tn,),
        in_specs=[pl.BlockSpec((rows, D), lambda j: (0, 0)),
                  pl.BlockSpec((D, tn), lambda j: (0, j)),
                  pl.BlockSpec((1, tn), lambda j: (0, j))],
        out_specs=pl.BlockSpec((rows, tn), lambda j: (0, j)),
        out_shape=jax.ShapeDtypeStruct((rows, n), F32),
        compiler_params=_params(("arbitrary",)),
        name="ada_mod",
    )(cp, w_ada, b_ada.reshape(1, n))
    return out[:B]


def _normmod_kernel(x_ref, g_ref, sc_ref, sh_ref, o_ref):
    x = x_ref[0]
    ms = jnp.mean(x * x, axis=-1, keepdims=True)
    y = x * lax.rsqrt(ms + EPS) * g_ref[...]
    o_ref[0] = (y * (1.0 + sc_ref[0]) + sh_ref[0]).astype(o_ref.dtype)


def _normmod(x, g, sc, sh):
    B, S, D = x.shape
    ts = _tile(S, 512)
    return pl.pallas_call(
        _normmod_kernel,
        grid=(B, S // ts),
        in_specs=[pl.BlockSpec((1, ts, D), lambda b, i: (b, i, 0)),
                  pl.BlockSpec((1, D), lambda b, i: (0, 0)),
                  pl.BlockSpec((1, 1, D), lambda b, i: (b, 0, 0)),
                  pl.BlockSpec((1, 1, D), lambda b, i: (b, 0, 0))],
        out_specs=pl.BlockSpec((1, ts, D), lambda b, i: (b, i, 0)),
        out_shape=jax.ShapeDtypeStruct((B, S, D), BF16),
        compiler_params=_params(("arbitrary", "arbitrary")),
        name="norm1_mod",
    )(x, g.reshape(1, D), sc.reshape(B, 1, D), sh.reshape(B, 1, D))


def _rope_tables(S, d):
    inv = ROPE_THETA ** (-jnp.arange(0, d, 2, dtype=F32) / d)
    ang = jnp.arange(S, dtype=F32)[:, None] * inv[None, :]
    cos = jnp.cos(ang)
    sin = jnp.sin(ang)
    reps = LANES // d
    cos_t = jnp.tile(jnp.concatenate([cos, cos], axis=1), (1, reps))
    sin_t = jnp.tile(jnp.concatenate([-sin, sin], axis=1), (1, reps))
    return cos_t, sin_t


def _rope128(y, cos, sin):
    return y * cos + pltpu.roll(y, 64, axis=1) * sin


def _rope64(y, cos, sin):
    lane = lax.broadcasted_iota(I32, y.shape, 1)
    rot = jnp.where((lane % 64) < 32, pltpu.roll(y, 96, axis=1), pltpu.roll(y, 32, axis=1))
    return y * cos + rot * sin


def _glu_kernel(x_ref, w_ref, o_ref):
    acc = jnp.dot(x_ref[...], w_ref[...], preferred_element_type=F32)
    half = acc.shape[1] // 2
    o_ref[...] = (acc[:, :half] * jax.nn.sigmoid(acc[:, half:])).astype(o_ref.dtype)


def _row_slabs(rows):
    step = min(rows, PROJ_SLAB)
    return [(r, step) for r in range(0, rows, step)]


def _q_kernel(x_ref, w_ref, cos_ref, sin_ref, g_ref, o_ref):
    g = g_ref[...]
    for r0, n in _row_slabs(x_ref.shape[0]):
        acc = jnp.dot(x_ref[r0:r0 + n, :], w_ref[...], preferred_element_type=F32)
        cos = cos_ref[r0:r0 + n, :]
        sin = sin_ref[r0:r0 + n, :]
        for h in range(acc.shape[1] // HEAD_DIM):
            xh = acc[:, h * HEAD_DIM:(h + 1) * HEAD_DIM]
            ms = jnp.mean(xh * xh, axis=-1, keepdims=True)
            y = xh * lax.rsqrt(ms + EPS) * g
            o_ref[r0:r0 + n, h * HEAD_DIM:(h + 1) * HEAD_DIM] = (
                _rope128(y, cos, sin) * (HEAD_DIM ** -0.5)).astype(o_ref.dtype)


def _kv_kernel(x_ref, w_ref, cos_ref, sin_ref, cos64_ref, sin64_ref, g_ref,
               k_ref, vt_ref, ika_ref, iw_ref):
    kvw = N_KV_HEADS * HEAD_DIM
    g = g_ref[...]
    for c in range(vt_ref.shape[0]):
        r0, n = c * KEY_CHUNK, KEY_CHUNK
        acc = jnp.dot(x_ref[r0:r0 + n, :], w_ref[...], preferred_element_type=F32)
        cos = cos_ref[r0:r0 + n, :]
        sin = sin_ref[r0:r0 + n, :]
        for h in range(N_KV_HEADS):
            xh = acc[:, h * HEAD_DIM:(h + 1) * HEAD_DIM]
            ms = jnp.mean(xh * xh, axis=-1, keepdims=True)
            y = xh * lax.rsqrt(ms + EPS) * g
            k_ref[r0:r0 + n, h * HEAD_DIM:(h + 1) * HEAD_DIM] = _rope128(y, cos, sin).astype(k_ref.dtype)
        vt_ref[c] = acc[:, kvw:2 * kvw].T.astype(vt_ref.dtype)
        tail = acc[:, 2 * kvw:2 * kvw + LANES]
        lane = lax.broadcasted_iota(I32, tail.shape, 1)
        roped = _rope64(tail, cos64_ref[r0:r0 + n, :], sin64_ref[r0:r0 + n, :])
        ika_ref[r0:r0 + n, :] = jnp.where(lane < IDX_DIM, roped, 0.0).astype(ika_ref.dtype)
        iw = pltpu.roll(tail, LANES - IDX_DIM, axis=1)
        iw_ref[r0:r0 + n, :] = jnp.where(lane < IDX_HEADS, iw * (IDX_HEADS ** -0.5), 0.0)


def _iq_kernel(x_ref, w_ref, cos_ref, sin_ref, o_ref):
    for r0, n in _row_slabs(x_ref.shape[0]):
        acc = jnp.dot(x_ref[r0:r0 + n, :], w_ref[...], preferred_element_type=F32)
        cos = cos_ref[r0:r0 + n, :]
        sin = sin_ref[r0:r0 + n, :]
        for p in range(acc.shape[1] // LANES):
            y = acc[:, p * LANES:(p + 1) * LANES]
            o_ref[r0:r0 + n, p * LANES:(p + 1) * LANES] = (
                _rope64(y, cos, sin) * (IDX_DIM ** -0.5)).astype(o_ref.dtype)


def _sigmoid_kernel(x_ref, w_ref, o_ref):
    acc = jnp.dot(x_ref[...], w_ref[...], preferred_element_type=F32)
    o_ref[...] = jax.nn.sigmoid(acc).astype(o_ref.dtype)


def _proj(kernel, h, w, tn, aux, aux_specs, out_shapes, out_specs, tm, name):
    N, D = h.shape
    n = w.shape[1]
    return pl.pallas_call(
        kernel,
        grid=(N // tm, n // tn),
        in_specs=[pl.BlockSpec((tm, D), lambda i, j: (i, 0)),
                  pl.BlockSpec((D, tn), lambda i, j: (0, j))] + aux_specs,
        out_specs=out_specs,
        out_shape=out_shapes,
        compiler_params=_params(("arbitrary", "arbitrary")),
        name=name,
    )(h, w, *aux)


def _conv_kernel(u_ref, halo_ref, cw_ref, cb_ref, lg_ref, lb_ref, wo_ref, gate_ref, o_ref,
                 buf_ref, sh_ref, conv_ref):
    i = pl.program_id(1)
    ts = u_ref.shape[1]
    SUBL = 8
    halo = halo_ref[0].astype(F32)
    buf_ref[0:CONV_HALO, :] = jnp.where(i == 0, 0.0, halo)
    buf_ref[CONV_HALO:, :] = u_ref[0].astype(F32)
    for b in range(1, SUBL):
        sh_ref[b - 1] = buf_ref[b:b + sh_ref.shape[1], :]
    rc = 32
    first = CONV_HALO - (CONV_TAPS - 1)

    def chunk(c, carry):
        r0 = pl.multiple_of(c * rc, rc)
        acc = jnp.zeros((rc, buf_ref.shape[1]), F32)
        for t in range(CONV_TAPS):
            a, b = divmod(first + t, SUBL)
            if b == 0:
                rows = buf_ref[pl.ds(r0 + a * SUBL, rc), :]
            else:
                rows = sh_ref[b - 1, pl.ds(r0 + a * SUBL, rc), :]
            acc = acc + cw_ref[t:t + 1, :] * rows
        conv_ref[pl.ds(r0, rc), :] = acc
        return carry

    lax.fori_loop(0, ts // rc, chunk, 0)
    u = conv_ref[...] + cb_ref[...]
    mu = jnp.mean(u, axis=-1, keepdims=True)
    uc = u - mu
    var = jnp.mean(uc * uc, axis=-1, keepdims=True)
    y = uc * lax.rsqrt(var + EPS) * lg_ref[...] + lb_ref[...]
    y = y * jax.nn.sigmoid(y)
    z = jnp.dot(y.astype(BF16), wo_ref[...], preferred_element_type=F32)
    o_ref[0] = (z * gate_ref[0].astype(F32)).astype(o_ref.dtype)


def _conv_branch(u, conv_w, conv_b, ln_g, ln_b, w_conv_out, gates):
    B, S, C = u.shape
    D = w_conv_out.shape[1]
    ts = _tile(S, 256)
    hb = ts // CONV_HALO
    cw = jnp.zeros((CONV_HALO, C), F32).at[:CONV_TAPS].set(conv_w)
    return pl.pallas_call(
        _conv_kernel,
        grid=(B, S // ts),
        in_specs=[pl.BlockSpec((1, ts, C), lambda b, i: (b, i, 0)),
                  pl.BlockSpec((1, CONV_HALO, C), lambda b, i: (b, jnp.maximum(i * hb - 1, 0), 0)),
                  pl.BlockSpec((CONV_HALO, C), lambda b, i: (0, 0)),
                  pl.BlockSpec((1, C), lambda b, i: (0, 0)),
                  pl.BlockSpec((1, C), lambda b, i: (0, 0)),
                  pl.BlockSpec((1, C), lambda b, i: (0, 0)),
                  pl.BlockSpec((C, D), lambda b, i: (0, 0)),
                  pl.BlockSpec((1, ts, D), lambda b, i: (b, i, 0))],
        out_specs=pl.BlockSpec((1, ts, D), lambda b, i: (b, i, 0)),
        out_shape=jax.ShapeDtypeStruct((B, S, D), BF16),
        scratch_shapes=[pltpu.VMEM((ts + CONV_HALO, C), F32),
                        pltpu.VMEM((7, ts + CONV_HALO - 8, C), F32),
                        pltpu.VMEM((ts, C), F32)],
        compiler_params=_params(("arbitrary", "arbitrary")),
        name="conv_branch",
    )(u, u, cw, conv_b.reshape(1, C), ln_g.reshape(1, C), ln_b.reshape(1, C),
      w_conv_out.astype(BF16), gates)


def _attn_kernel(n_sel, iq_ref, iw_ref, q_ref, ika_ref, k_ref, vt_ref, o_ref,
                 key_ref, iqt_ref, wt_ref, qt_ref, bound_ref, s_ref, acc_ref):
    i = pl.program_id(1)
    QB = Q_BLOCK
    CK = KEY_CHUNK
    GW = GROUP * QB
    SUBL = 8
    nck = (i * QB + QB + CK - 1) // CK
    q_pos = i * QB + lax.broadcasted_iota(I32, (CK, QB), 1)
    krow = lax.broadcasted_iota(I32, (CK, QB), 0)
    row8 = krow[0:SUBL]

    iq = iq_ref[...].astype(F32)
    iqt_ref[IDX_DIM:, :] = jnp.zeros((LANES - IDX_DIM, IDX_HEADS * QB), BF16)
    for p in range(IDX_HEADS // 2):
        t = iq[:, p * LANES:(p + 1) * LANES].T
        iqt_ref[0:IDX_DIM, (2 * p) * QB:(2 * p + 1) * QB] = t[0:IDX_DIM].astype(BF16)
        iqt_ref[0:IDX_DIM, (2 * p + 1) * QB:(2 * p + 2) * QB] = t[IDX_DIM:].astype(BF16)
    wt_ref[...] = iw_ref[...].T
    qf = q_ref[...].astype(F32)
    for h in range(N_HEADS):
        qt_ref[:, h * QB:(h + 1) * QB] = qf[:, h * HEAD_DIM:(h + 1) * HEAD_DIM].T.astype(BF16)

    def score_chunk(c, carry):
        k0 = pl.multiple_of(c * CK, CK)
        ka = ika_ref[pl.ds(k0, CK), :]
        acc = jnp.zeros((CK, QB), F32)
        for hg in range(IDX_HEADS // 4):
            st = jnp.dot(ka, iqt_ref[:, hg * 4 * QB:(hg + 1) * 4 * QB], preferred_element_type=F32)
            for j in range(4):
                h = hg * 4 + j
                acc = acc + jnp.maximum(st[:, j * QB:(j + 1) * QB], 0.0) * wt_ref[h:h + 1, :]
        key_ref[c] = jnp.where(k0 + krow <= q_pos, acc, -jnp.inf)
        return carry

    lax.fori_loop(0, nck, score_chunk, 0)

    def count(pred):
        def body(c, cnts):
            cnts = list(cnts)
            for r in range(CK // SUBL):
                sc = key_ref[c, r * SUBL:(r + 1) * SUBL, :]
                kidx = c * CK + r * SUBL + row8
                cnts[r % 4] = cnts[r % 4] + pred(sc, kidx).astype(I32)
            return tuple(cnts)
        z = jnp.zeros((SUBL, QB), I32)
        cnts = lax.fori_loop(0, nck, body, (z, z, z, z))
        tot = (cnts[0] + cnts[1]) + (cnts[2] + cnts[3])
        return jnp.broadcast_to(jnp.sum(tot, axis=0, keepdims=True), (SUBL, QB))

    def code_value(code):
        return pltpu.bitcast(code ^ ((code >> 31) & 0x7FFFFFFF), F32)

    zero = jnp.zeros((SUBL, QB), I32)
    zero_f = jnp.zeros((SUBL, QB), F32)
    code = jnp.where(count(lambda sc, kidx: sc >= zero_f) >= n_sel, zero, INT_MIN)

    def bit_step(b, code):
        cand = code | (1 << (30 - b))
        cand_f = code_value(cand)
        return jnp.where(count(lambda sc, kidx: sc >= cand_f) >= n_sel, cand, code)

    code = lax.fori_loop(0, 31, bit_step, code)
    thr = jnp.where(code == INT_MIN, -jnp.inf, code_value(code))

    n_gt = count(lambda sc, kidx: sc > thr)
    n_ge = count(lambda sc, kidx: sc >= thr)
    need = n_sel - n_gt
    bound_ref[...] = jnp.full((SUBL, QB), 2 ** 30, I32)

    @pl.when(jnp.max(jnp.where(code > INT_MIN, n_ge, 0)) > n_sel)
    def _():
        def idx_step(b, bound):
            cand = bound | (1 << (14 - b))
            n_tie = count(lambda sc, kidx: (sc == thr) & (kidx < cand))
            return jnp.where(n_tie <= need, cand, bound)
        bound_ref[...] = lax.fori_loop(0, 15, idx_step, zero)

    bound1 = bound_ref[0:1, :]
    thr1 = thr[0:1, :]

    def logit_chunk(c, mx):
        k0 = pl.multiple_of(c * CK, CK)
        sc = key_ref[c]
        kidx = k0 + krow
        sel = ((sc > thr1) | ((sc == thr1) & (kidx < bound1))) & (kidx <= q_pos)
        bias1 = jnp.where(sel, 0.0, NEG_BIG)
        bias = jnp.concatenate([bias1] * GROUP, axis=1)
        out = []
        for kh in range(N_KV_HEADS):
            kc = k_ref[pl.ds(k0, CK), kh * HEAD_DIM:(kh + 1) * HEAD_DIM]
            st = jnp.dot(kc, qt_ref[:, kh * GW:(kh + 1) * GW], preferred_element_type=F32) + bias
            s_ref[c, :, kh * GW:(kh + 1) * GW] = st
            m8 = mx[kh]
            for r in range(CK // SUBL):
                m8 = jnp.maximum(m8, st[r * SUBL:(r + 1) * SUBL, :])
            out.append(m8)
        return tuple(out)

    mx = lax.fori_loop(0, nck, logit_chunk,
                       tuple(jnp.full((SUBL, GW), NEG_BIG, F32) for _ in range(N_KV_HEADS)))
    m_fin = [jnp.max(m8, axis=0, keepdims=True) for m8 in mx]

    acc_ref[...] = jnp.zeros(acc_ref.shape, F32)

    def value_chunk(c, ls):
        out = []
        for kh in range(N_KV_HEADS):
            p = jnp.exp(s_ref[c, :, kh * GW:(kh + 1) * GW] - m_fin[kh])
            l8 = ls[kh]
            for r in range(CK // SUBL):
                l8 = l8 + p[r * SUBL:(r + 1) * SUBL, :]
            out.append(l8)
            vt = vt_ref[c, kh * HEAD_DIM:(kh + 1) * HEAD_DIM, :]
            acc_ref[kh] += jnp.dot(vt, p.astype(BF16), preferred_element_type=F32)
        return tuple(out)

    ls = lax.fori_loop(0, nck, value_chunk,
                       tuple(jnp.zeros((SUBL, GW), F32) for _ in range(N_KV_HEADS)))
    for kh in range(N_KV_HEADS):
        ot = acc_ref[kh] / jnp.sum(ls[kh], axis=0, keepdims=True)
        for g in range(GROUP):
            h = kh * GROUP + g
            o_ref[:, h * HEAD_DIM:(h + 1) * HEAD_DIM] = ot[:, g * QB:(g + 1) * QB].T.astype(o_ref.dtype)


def _sparse_attention(iq, iw, q, ika, k, vt, B, S):
    N = B * S
    nb = S // Q_BLOCK
    n_sel = min(IDX_TOPK_MAX, S // 4)
    kvw = N_KV_HEADS * HEAD_DIM
    nchunks = S // KEY_CHUNK
    gw = GROUP * Q_BLOCK
    qspec = lambda w: pl.BlockSpec((Q_BLOCK, w), lambda b, i: (b * nb + i, 0))
    kspec = lambda w: pl.BlockSpec((S, w), lambda b, i: (b, 0))
    return pl.pallas_call(
        functools.partial(_attn_kernel, n_sel),
        grid=(B, nb),
        in_specs=[qspec(IDX_WIDTH), qspec(LANES), qspec(ATTN_WIDTH), kspec(LANES), kspec(kvw),
                  pl.BlockSpec((nchunks, kvw, KEY_CHUNK), lambda b, i: (b, 0, 0))],
        out_specs=qspec(ATTN_WIDTH),
        out_shape=jax.ShapeDtypeStruct((N, ATTN_WIDTH), BF16),
        scratch_shapes=[pltpu.VMEM((nchunks, KEY_CHUNK, Q_BLOCK), F32),
                        pltpu.VMEM((LANES, IDX_HEADS * Q_BLOCK), BF16),
                        pltpu.VMEM((LANES, Q_BLOCK), F32),
                        pltpu.VMEM((HEAD_DIM, N_HEADS * Q_BLOCK), BF16),
                        pltpu.VMEM((8, Q_BLOCK), I32),
                        pltpu.VMEM((nchunks, KEY_CHUNK, N_KV_HEADS * gw), F32),
                        pltpu.VMEM((N_KV_HEADS, HEAD_DIM, gw), F32)],
        compiler_params=_params(("arbitrary", "arbitrary")),
        name="dsa_attention",
    )(iq, iw, q, ika, k, vt)


def _split_bf16(a):
    hi = a.astype(BF16)
    lo = (a - hi.astype(F32)).astype(BF16)
    return hi, lo


def _mix_kernel(yc_ref, ao_ref, gate_ref, x_ref, g1_ref, wao_ref, wo_ref, n2_ref, sc2_ref, sh2_ref,
                wrh_ref, wrl_ref, br_ref, x1_ref, h2_ref, idx_ref, prob_ref):
    for r0, n in [(0, x_ref.shape[0])]:
        rows = slice(r0, r0 + n)
        ya = jnp.dot(ao_ref[rows, :], wao_ref[...], preferred_element_type=F32)
        mixed = yc_ref[rows, :].astype(F32) + gate_ref[rows, :].astype(F32) * ya
        z = jnp.dot(mixed.astype(BF16), wo_ref[...], preferred_element_type=F32)
        x1 = x_ref[rows, :] + g1_ref[0] * z
        x1_ref[rows, :] = x1
        ms = jnp.mean(x1 * x1, axis=-1, keepdims=True)
        h2 = x1 * lax.rsqrt(ms + EPS) * n2_ref[...] * (1.0 + sc2_ref[0]) + sh2_ref[0]
        h2_ref[rows, :] = h2.astype(h2_ref.dtype)
        hh, hl = _split_bf16(h2)
        logits = (jnp.dot(hh, wrh_ref[...], preferred_element_type=F32)
                  + jnp.dot(hh, wrl_ref[...], preferred_element_type=F32)
                  + jnp.dot(hl, wrh_ref[...], preferred_element_type=F32)) + br_ref[...]
        lane = lax.broadcasted_iota(I32, logits.shape, 1)
        work = jnp.where(lane < N_EXPERTS, logits, -jnp.inf)
        vals = []
        idx_out = jnp.zeros(logits.shape, I32)
        for k in range(TOP_K):
            mv = jnp.max(work, axis=-1, keepdims=True)
            mi = jnp.min(jnp.where(work == mv, lane, LANES), axis=-1, keepdims=True)
            idx_out = jnp.where(lane == k, mi, idx_out)
            work = jnp.where(lane == mi, -jnp.inf, work)
            vals.append(mv)
        es = [jnp.exp(vk - vals[0]) for vk in vals]
        den = es[0]
        for e in es[1:]:
            den = den + e
        prob_out = jnp.zeros(logits.shape, F32)
        for k in range(TOP_K):
            prob_out = jnp.where(lane == k, es[k] / den, prob_out)
        idx_ref[rows, :] = idx_out
        prob_ref[rows, :] = prob_out


def _mix(yc, ao, gates, x, g1, w_attn_out, w_out, norm2_g, sc2, sh2, w_router, b_router, B, S):
    N, D = x.shape
    tm = _tile(S, 512)
    nS = S // tm
    wr = jnp.zeros((D, LANES), F32).at[:, :N_EXPERTS].set(w_router)
    wrh, wrl = _split_bf16(wr)
    br = jnp.zeros((1, LANES), F32).at[0, :N_EXPERTS].set(b_router)
    row = lambda w, c=0: pl.BlockSpec((tm, w), lambda i: (i, c))
    full = lambda a: pl.BlockSpec(a.shape, lambda i: (0,) * a.ndim)
    per_b = pl.BlockSpec((1, 1, D), lambda i: (i // nS, 0, 0))
    wao = w_attn_out.astype(BF16)
    wo = w_out.astype(BF16)
    n2 = norm2_g.reshape(1, D)
    return pl.pallas_call(
        _mix_kernel,
        grid=(N // tm,),
        in_specs=[row(D), row(ATTN_WIDTH), row(D, 1), row(D), per_b, full(wao), full(wo),
                  full(n2), per_b, per_b, full(wrh), full(wrl), full(br)],
        out_specs=[row(D), row(D), row(LANES), row(LANES)],
        out_shape=[jax.ShapeDtypeStruct((N, D), F32), jax.ShapeDtypeStruct((N, D), BF16),
                   jax.ShapeDtypeStruct((N, LANES), I32), jax.ShapeDtypeStruct((N, LANES), F32)],
        compiler_params=_params(("arbitrary",)),
        name="mix_router",
    )(yc, ao, gates, x, g1.reshape(B, 1, D), wao, wo, n2, sc2.reshape(B, 1, D),
      sh2.reshape(B, 1, D), wrh, wrl, br)


def _moe_kernel(blk_ref, exp_ref, nsub_ref, x_ref, wg_ref, bg_ref, wu_ref, bu_ref,
                wd_ref, bd_ref, o_ref, wg_s, wu_s, wd_s, acc_ref):
    w = pl.program_id(0)
    f = pl.program_id(1)
    nf = pl.num_programs(1)
    nsub = nsub_ref[w]
    n_sub_max = o_ref.shape[0] // MOE_SUB

    def zero_rows(s, carry):
        r0 = pl.multiple_of(s * MOE_SUB, MOE_SUB)
        o_ref[pl.ds(r0, MOE_SUB), :] = jnp.zeros((MOE_SUB, o_ref.shape[1]), o_ref.dtype)
        return carry

    @pl.when((nsub == 0) & (f == 0))
    def _():
        lax.fori_loop(0, n_sub_max, zero_rows, 0)

    @pl.when(nsub > 0)
    def _():
        @pl.when(f == 0)
        def _():
            def zero(s, carry):
                r0 = pl.multiple_of(s * MOE_SUB, MOE_SUB)
                acc_ref[pl.ds(r0, MOE_SUB), :] = jnp.zeros((MOE_SUB, acc_ref.shape[1]), F32)
                return carry
            lax.fori_loop(0, nsub, zero, 0)

        def mlp_rows(r0, rows):
            xs = x_ref[pl.ds(r0, rows), :]
            g = jnp.dot(xs, wg_s[...], preferred_element_type=F32) + bg_ref[0]
            u = jnp.dot(xs, wu_s[...], preferred_element_type=F32) + bu_ref[0]
            g = jnp.minimum(g, SWIGLU_LIMIT)
            u = jnp.clip(u, -SWIGLU_LIMIT, SWIGLU_LIMIT)
            act = g * jax.nn.sigmoid(SWIGLU_ALPHA * g) * (u + 1.0)
            acc_ref[pl.ds(r0, rows), :] += jnp.dot(act.astype(BF16), wd_s[...],
                                                   preferred_element_type=F32)

        xs0 = x_ref[0:MOE_SUB, :]
        tile = 2 * LANES
        g0 = jnp.zeros((MOE_SUB, wg_s.shape[1]), F32) + bg_ref[0]
        u0 = jnp.zeros((MOE_SUB, wu_s.shape[1]), F32) + bu_ref[0]
        for kt in range(xs0.shape[1] // tile):
            ks = slice(kt * tile, (kt + 1) * tile)
            wgt = wg_ref[0, ks, :].astype(BF16)
            wut = wu_ref[0, ks, :].astype(BF16)
            wg_s[ks, :] = wgt
            wu_s[ks, :] = wut
            g0 = g0 + jnp.dot(xs0[:, ks], wgt, preferred_element_type=F32)
            u0 = u0 + jnp.dot(xs0[:, ks], wut, preferred_element_type=F32)
        g0 = jnp.minimum(g0, SWIGLU_LIMIT)
        u0 = jnp.clip(u0, -SWIGLU_LIMIT, SWIGLU_LIMIT)
        act0 = (g0 * jax.nn.sigmoid(SWIGLU_ALPHA * g0) * (u0 + 1.0)).astype(BF16)
        for nt in range(acc_ref.shape[1] // tile):
            ns = slice(nt * tile, (nt + 1) * tile)
            wdt = wd_ref[0, :, ns].astype(BF16)
            wd_s[:, ns] = wdt
            acc_ref[0:MOE_SUB, ns] += jnp.dot(act0, wdt, preferred_element_type=F32)

        rest = nsub - 1

        def pair(s, carry):
            mlp_rows(pl.multiple_of(MOE_SUB + s * (2 * MOE_SUB), MOE_SUB), 2 * MOE_SUB)
            return carry
        lax.fori_loop(0, rest // 2, pair, 0)

        @pl.when(rest % 2 == 1)
        def _():
            mlp_rows(pl.multiple_of(nsub * MOE_SUB - MOE_SUB, MOE_SUB), MOE_SUB)

        @pl.when(f == nf - 1)
        def _():
            def fin(s, carry):
                r0 = pl.multiple_of(s * MOE_SUB, MOE_SUB)
                y = acc_ref[pl.ds(r0, MOE_SUB), :] + bd_ref[0]
                o_ref[pl.ds(r0, MOE_SUB), :] = y.astype(o_ref.dtype)
                return carry
            lax.fori_loop(0, nsub, fin, 0)
            lax.fori_loop(nsub, n_sub_max, zero_rows, 0)


def _moe_experts(xs, item_blk, item_exp, item_nsub, w_gate, b_gate, w_up, b_up,
                 w_down, b_down):
    P, D = xs.shape
    E, _, F = w_gate.shape
    n_items = item_blk.shape[0]
    tf = _tile(F, MOE_FCHUNK)
    nf = F // tf

    def fidx(f, ns):
        return jnp.where(ns > 0, f, nf - 1)

    grid_spec = pltpu.PrefetchScalarGridSpec(
        num_scalar_prefetch=3,
        grid=(n_items, nf),
        in_specs=[
            pl.BlockSpec((MOE_TILE, D), lambda w, f, blk, ex, ns: (blk[w], 0)),
            pl.BlockSpec((1, D, tf), lambda w, f, blk, ex, ns: (ex[w], 0, fidx(f, ns[w]))),
            pl.BlockSpec((1, 1, tf), lambda w, f, blk, ex, ns: (ex[w], 0, fidx(f, ns[w]))),
            pl.BlockSpec((1, D, tf), lambda w, f, blk, ex, ns: (ex[w], 0, fidx(f, ns[w]))),
            pl.BlockSpec((1, 1, tf), lambda w, f, blk, ex, ns: (ex[w], 0, fidx(f, ns[w]))),
            pl.BlockSpec((1, tf, D), lambda w, f, blk, ex, ns: (ex[w], fidx(f, ns[w]), 0)),
            pl.BlockSpec((1, 1, D), lambda w, f, blk, ex, ns: (ex[w], 0, 0)),
        ],
        out_specs=pl.BlockSpec((MOE_TILE, D), lambda w, f, blk, ex, ns: (w, 0)),
        scratch_shapes=[pltpu.VMEM((D, tf), BF16), pltpu.VMEM((D, tf), BF16),
                        pltpu.VMEM((tf, D), BF16), pltpu.VMEM((MOE_TILE, D), F32)],
    )
    return pl.pallas_call(
        _moe_kernel,
        grid_spec=grid_spec,
        out_shape=jax.ShapeDtypeStruct((P, D), BF16),
        compiler_params=_params(("arbitrary", "arbitrary")),
        name="moe_experts",
    )(item_blk, item_exp, item_nsub, xs, w_gate, b_gate.reshape(E, 1, F), w_up,
      b_up.reshape(E, 1, F), w_down, b_down.reshape(E, 1, D))


def _route(top_idx):
    N = top_idx.shape[0]
    A = N * TOP_K
    n_items = A // MOE_TILE + N_EXPERTS
    P = n_items * MOE_TILE
    flat_e = top_idx.reshape(A)
    order = jnp.argsort(flat_e).astype(I32)
    pos = jnp.argsort(order).astype(I32)
    eids = jnp.arange(N_EXPERTS, dtype=I32)
    counts = jnp.sum((flat_e[:, None] == eids[None, :]).astype(I32), axis=0)
    grp_start = jnp.cumsum(counts) - counts
    n_it = (counts + MOE_TILE - 1) // MOE_TILE
    it_end = jnp.cumsum(n_it)
    it_start = it_end - n_it
    per_item = (counts + jnp.maximum(n_it, 1) - 1) // jnp.maximum(n_it, 1)
    per_item = jnp.maximum((per_item + MOE_SUB - 1) // MOE_SUB * MOE_SUB, MOE_SUB)
    n_used = it_end[-1]
    items = jnp.arange(n_items, dtype=I32)
    last = jnp.maximum(n_used - 1, 0)
    item_blk = jnp.minimum(items, last)
    item_exp = jnp.minimum(jnp.sum((item_blk[:, None] >= it_end[None, :]).astype(I32), axis=1),
                           N_EXPERTS - 1)
    k_in = item_blk - it_start[item_exp]
    first = grp_start[item_exp] + k_in * per_item[item_exp]
    nvalid = jnp.clip(counts[item_exp] - k_in * per_item[item_exp], 0, per_item[item_exp])
    nvalid = jnp.where(items < n_used, nvalid, 0)
    item_nsub = ((nvalid + MOE_SUB - 1) // MOE_SUB).astype(I32)
    t = jnp.arange(MOE_TILE, dtype=I32)
    valid = t[None, :] < nvalid[:, None]
    src = order[jnp.clip(first[:, None] + t[None, :], 0, A - 1)]
    rows = jnp.arange(P, dtype=I32).reshape(n_items, MOE_TILE)
    row_tok = jnp.where(valid, src // TOP_K, rows % N).reshape(P).astype(I32)
    off = pos - grp_start[flat_e]
    kk = off // per_item[flat_e]
    dest = (it_start[flat_e] + kk) * MOE_TILE + off - kk * per_item[flat_e]
    return row_tok, dest.astype(I32), item_blk, item_exp, item_nsub


def _final_kernel(x1_ref, y_ref, p_ref, g2_ref, o_ref):
    p = p_ref[...]
    y = y_ref[0].astype(F32) * p[:, 0:1]
    for k in range(1, TOP_K):
        y = y + y_ref[k].astype(F32) * p[:, k:k + 1]
    o_ref[...] = x1_ref[...] + g2_ref[0] * y


def _final(x1, yk, probs, g2, B, S):
    N, D = x1.shape
    tm = _tile(S, 512)
    nS = S // tm
    return pl.pallas_call(
        _final_kernel,
        grid=(N // tm,),
        in_specs=[pl.BlockSpec((tm, D), lambda i: (i, 0)),
                  pl.BlockSpec((TOP_K, tm, D), lambda i: (0, i, 0)),
                  pl.BlockSpec((tm, LANES), lambda i: (i, 0)),
                  pl.BlockSpec((1, 1, D), lambda i: (i // nS, 0, 0))],
        out_specs=pl.BlockSpec((tm, D), lambda i: (i, 0)),
        out_shape=jax.ShapeDtypeStruct((N, D), F32),
        compiler_params=_params(("arbitrary",)),
        name="moe_combine",
    )(x1, yk, probs, g2.reshape(B, 1, D))


def _layer(x, c, w_ada, b_ada, norm1_g, w_in, conv_w, conv_b, conv_ln_g, conv_ln_b, w_conv_out,
           q_norm_g, k_norm_g, w_attn_out, w_out, norm2_g, w_router, b_router, w_gate, b_gate,
           w_up, b_up, w_down, b_down):
    B, S, D = x.shape
    N = B * S
    C = conv_w.shape[1]
    kvw = N_KV_HEADS * HEAD_DIM

    mod = _ada(c, w_ada, b_ada)
    sh1, sc1, g1, sh2, sc2, g2 = [mod[:, j * D:(j + 1) * D] for j in range(6)]

    h = _normmod(x, norm1_g, sc1, sh1).reshape(N, D)

    o = 0
    w_conv_in = w_in[:, o:o + 2 * C]; o += 2 * C
    w_q = w_in[:, o:o + ATTN_WIDTH]; o += ATTN_WIDTH
    w_kv_small = w_in[:, o:o + 2 * kvw + IDX_WIDTH + IDX_DIM + IDX_HEADS]
    w_k_v = w_in[:, o:o + 2 * kvw]; o += 2 * kvw
    w_iq = w_in[:, o:o + IDX_WIDTH]; o += IDX_WIDTH
    w_ik_iw = w_in[:, o:o + IDX_DIM + IDX_HEADS]; o += IDX_DIM + IDX_HEADS
    w_gates = w_in[:, o:o + 2 * D]
    del w_kv_small

    tm = _tile(S, 1024)
    nS = S // tm
    cos128, sin128 = _rope_tables(S, HEAD_DIM)
    cos64, sin64 = _rope_tables(S, IDX_DIM)
    tab = pl.BlockSpec((tm, LANES), lambda i, j: (i % nS, 0))
    vec = pl.BlockSpec((1, LANES), lambda i, j: (0, 0))

    tn = _tile(2 * C, 1024)
    hc = tn // 2
    wv = w_conv_in[:, :C].reshape(D, C // hc, hc)
    wg = w_conv_in[:, C:].reshape(D, C // hc, hc)
    w_glu = jnp.concatenate([wv, wg], axis=2).reshape(D, 2 * C).astype(BF16)
    u = _proj(_glu_kernel, h, w_glu, tn, [], [],
              jax.ShapeDtypeStruct((N, C), BF16),
              pl.BlockSpec((tm, hc), lambda i, j: (i, j)), tm, "proj_conv_glu")

    tq = _tile(ATTN_WIDTH, 512)
    q = _proj(_q_kernel, h, w_q.astype(BF16), tq, [cos128, sin128, q_norm_g.reshape(1, HEAD_DIM)],
              [tab, tab, vec], jax.ShapeDtypeStruct((N, ATTN_WIDTH), BF16),
              pl.BlockSpec((tm, tq), lambda i, j: (i, j)), tm, "proj_q")

    pad = LANES - IDX_DIM - IDX_HEADS
    w_kvs = jnp.concatenate([w_k_v, w_ik_iw, jnp.zeros((D, pad), F32)], axis=1).astype(BF16)
    nk = w_kvs.shape[1]
    blk = lambda w: pl.BlockSpec((tm, w), lambda i, j: (i, 0))
    cpt = tm // KEY_CHUNK
    k, vt, ika, iw = _proj(
        _kv_kernel, h, w_kvs, nk,
        [cos128, sin128, cos64, sin64, k_norm_g.reshape(1, HEAD_DIM)], [tab, tab, tab, tab, vec],
        [jax.ShapeDtypeStruct((N, kvw), BF16), jax.ShapeDtypeStruct((N // KEY_CHUNK, kvw, KEY_CHUNK), BF16),
         jax.ShapeDtypeStruct((N, LANES), BF16), jax.ShapeDtypeStruct((N, LANES), F32)],
        [blk(kvw), pl.BlockSpec((cpt, kvw, KEY_CHUNK), lambda i, j: (i, 0, 0)), blk(LANES), blk(LANES)],
        tm, "proj_kv_indexer_key")

    ti = _tile(IDX_WIDTH, 512)
    iq = _proj(_iq_kernel, h, w_iq.astype(BF16), ti, [cos64, sin64], [tab, tab],
               jax.ShapeDtypeStruct((N, IDX_WIDTH), BF16),
               pl.BlockSpec((tm, ti), lambda i, j: (i, j)), tm, "proj_iq")

    tg = _tile(2 * D, 1024)
    gates = _proj(_sigmoid_kernel, h, w_gates.astype(BF16), tg, [], [],
                  jax.ShapeDtypeStruct((N, 2 * D), BF16),
                  pl.BlockSpec((tm, tg), lambda i, j: (i, j)), tm, "proj_gates")

    yc = _conv_branch(u.reshape(B, S, C), conv_w, conv_b, conv_ln_g, conv_ln_b, w_conv_out,
                      gates.reshape(B, S, 2 * D)).reshape(N, D)
    ao = _sparse_attention(iq, iw, q, ika, k, vt, B, S)

    x1, h2, top_idx, probs = _mix(yc, ao, gates, x.reshape(N, D), g1, w_attn_out, w_out, norm2_g,
                                  sc2, sh2, w_router, b_router, B, S)

    row_tok, dest, item_blk, item_exp, item_nsub = _route(top_idx[:, :TOP_K])
    xs = h2.at[row_tok].get(mode="promise_in_bounds")
    ys = _moe_experts(xs, item_blk, item_exp, item_nsub, w_gate, b_gate, w_up, b_up, w_down,
                      b_down)
    yk = ys.at[dest.reshape(N, TOP_K).T].get(mode="promise_in_bounds")
    out = _final(x1, yk, probs, g2, B, S)
    return out.reshape(B, S, D)


def kernel(x, c, w_ada, b_ada, norm1_g, w_in, conv_w, conv_b, conv_ln_g, conv_ln_b, w_conv_out,
           q_norm_g, k_norm_g, w_attn_out, w_out, norm2_g, w_router, b_router, w_gate, b_gate,
           w_up, b_up, w_down, b_down):
    for l in range(w_ada.shape[0]):
        x = _layer(x, c, w_ada[l], b_ada[l], norm1_g[l], w_in[l], conv_w[l], conv_b[l],
                   conv_ln_g[l], conv_ln_b[l], w_conv_out[l], q_norm_g[l], k_norm_g[l],
                   w_attn_out[l], w_out[l], norm2_g[l], w_router[l], b_router[l], w_gate[l],
                   b_gate[l], w_up[l], b_up[l], w_down[l], b_down[l])
    return x
```
